```python
import functools
import jax
import jax.numpy as jnp
from jax import lax
import numpy as np

D_MODEL = 4096
BATCH = 4
SEQ = 2048
DEPTH = 4
DEC_BATCH = 128
DEC_SEQ = 4
PAST_LEN = 8192
PAGE_SIZE = 128

GLA_DK = 128
GLA_DV = 256
GLA_HEADS = (D_MODEL // 2) // GLA_DV
GLA_GATE_RANK = 16
GLA_GATE_TAU = 16.0
GLA_CHUNK = 64
MLA_D_NOPE = 128
MLA_D_ROPE = 64
MLA_D_V = 128
MLA_HEADS = (D_MODEL // 2) // MLA_D_V
MLA_Q_LORA = 768
MLA_KV_LORA = 256
MLA_SCALE = (MLA_D_NOPE + MLA_D_ROPE) ** -0.5
ROPE_THETA = 10000.0
Q_BLOCK = 128
N_EXPERTS = 16
N_GROUPS = 4
EXPERTS_PER_GROUP = N_EXPERTS // N_GROUPS
TOP_K = 2
D_FF_EXPERT = 1024
LN_EPS = 1e-5
RMS_EPS = 1e-6
DEEPNORM_ALPHA = (2.0 * DEPTH) ** 0.25
DEEPNORM_BETA = (8.0 * DEPTH) ** -0.25
IN_SIZES = (GLA_HEADS * GLA_DK, GLA_HEADS * GLA_DK, GLA_HEADS * GLA_DV, GLA_GATE_RANK,
            GLA_HEADS * GLA_DV, MLA_Q_LORA, MLA_KV_LORA, MLA_D_ROPE)
D_IN = GLA_HEADS * (2 * GLA_DK + 2 * GLA_DV) + GLA_GATE_RANK + MLA_Q_LORA + MLA_KV_LORA + MLA_D_ROPE

kernel_name = 'hymba_gla_mla_grouped_moe_deepnorm_step'


def layer_norm(x, g, b):
    xf = x.astype(jnp.float32)
    mu = xf.mean(-1, keepdims=True)
    var = jnp.square(xf - mu).mean(-1, keepdims=True)
    return ((xf - mu) * lax.rsqrt(var + LN_EPS) * g + b).astype(x.dtype)


def rms_norm(x, g):
    xf = x.astype(jnp.float32)
    return (xf * lax.rsqrt(jnp.square(xf).mean(-1, keepdims=True) + RMS_EPS) * g).astype(x.dtype)


def rope_cos_sin(pos):
    half = MLA_D_ROPE // 2
    inv_freq = ROPE_THETA ** (-(jnp.arange(half, dtype=jnp.float32) * 2.0) / MLA_D_ROPE)
    ang = pos.astype(jnp.float32)[:, None] * inv_freq[None, :]
    return jnp.cos(ang), jnp.sin(ang)


def apply_rope(x, cos, sin):
    half = MLA_D_ROPE // 2
    xf = x.astype(jnp.float32)
    x1, x2 = xf[..., :half], xf[..., half:]
    return jnp.concatenate([x1 * cos - x2 * sin, x2 * cos + x1 * sin], axis=-1).astype(x.dtype)


def split_projection(h):
    offsets = []
    acc = 0
    for s in IN_SIZES[:-1]:
        acc += s
        offsets.append(acc)
    return jnp.split(h, offsets, axis=-1)


def gla_chunk(S0, q, k, v, lg):
    qf, kf, vf = q.astype(jnp.float32), k.astype(jnp.float32), v.astype(jnp.float32)
    C = q.shape[1]
    b = jnp.cumsum(lg, axis=1)
    causal = jnp.tril(jnp.ones((C, C), dtype=bool))
    diff = b[:, :, None] - b[:, None, :]
    decay = jnp.exp(jnp.where(causal[None, :, :, None, None], diff, -jnp.inf))
    attn = jnp.einsum('bthd,bshd,btshd->bhts', qf, kf, decay)
    o = jnp.einsum('bhts,bshv->bthv', attn, vf) + jnp.einsum('bthd,bhdv->bthv', qf * jnp.exp(b), S0)
    b_last = b[:, -1]
    k_dec = kf * jnp.exp(b_last[:, None] - b)
    S_new = S0 * jnp.exp(b_last)[..., None] + jnp.einsum('bshd,bshv->bhdv', k_dec, vf)
    return S_new, o


def gla_recurrence(S0, q, k, v, lg, chunk):
    B, T = q.shape[0], q.shape[1]
    nc = T // chunk

    def to_chunks(a):
        return jnp.moveaxis(a.reshape((B, nc, chunk) + a.shape[2:]), 1, 0)

    S_fin, o = lax.scan(lambda S, c: gla_chunk(S, *c), S0,
                        (to_chunks(q), to_chunks(k), to_chunks(v), to_chunks(lg)))
    return S_fin, jnp.moveaxis(o, 0, 1).reshape(B, T, GLA_HEADS, GLA_DV)


def gla_mixer(gq, gk, gv, g_lr, g_out, S0, chunk, w_gk_up, b_gk, gla_norm_g):
    B, T, _ = gq.shape
    q = gq.reshape(B, T, GLA_HEADS, GLA_DK) * GLA_DK ** -0.5
    k = gk.reshape(B, T, GLA_HEADS, GLA_DK)
    v = gv.reshape(B, T, GLA_HEADS, GLA_DV)
    lg = jax.nn.log_sigmoid((g_lr @ w_gk_up + b_gk).astype(jnp.float32)) / GLA_GATE_TAU
    lg = lg.reshape(B, T, GLA_HEADS, GLA_DK)
    S_fin, o = gla_recurrence(S0, q, k, v, lg, chunk)
    o = rms_norm(o, gla_norm_g.reshape(GLA_HEADS, GLA_DV)).astype(gq.dtype).reshape(B, T, -1)
    return o * jax.nn.silu(g_out), S_fin


def mla_project(c_q, c_kv, kr_raw, cos, sin, q_norm_g, kv_norm_g, w_q_up):
    q = jnp.einsum('btc,chd->bthd', rms_norm(c_q, q_norm_g), w_q_up)
    q_nope = q[..., :MLA_D_NOPE]
    q_rope = apply_rope(q[..., MLA_D_NOPE:], cos[:, None, :], sin[:, None, :])
    c_kv_n = rms_norm(c_kv, kv_norm_g)
    k_rope = apply_rope(kr_raw, cos, sin)
    return q_nope, q_rope, c_kv_n, k_rope


def mla_prompt_attention(q_nope, q_rope, c_kv, k_rope, w_uk, w_uv):
    B, S = q_nope.shape[0], q_nope.shape[1]
    k_nope = jnp.einsum('bsc,chd->bshd', c_kv, w_uk)
    v = jnp.einsum('bsc,chd->bshd', c_kv, w_uv)
    nb = S // Q_BLOCK
    qn = q_nope.reshape(B, nb, Q_BLOCK, MLA_HEADS, MLA_D_NOPE).swapaxes(0, 1)
    qr = q_rope.reshape(B, nb, Q_BLOCK, MLA_HEADS, MLA_D_ROPE).swapaxes(0, 1)
    key_pos = jnp.arange(S)

    def query_block(args):
        i, qn_b, qr_b = args
        s = jnp.einsum('bqhd,bkhd->bhqk', qn_b, k_nope) + jnp.einsum('bqhr,bkr->bhqk', qr_b, k_rope)
        s = s.astype(jnp.float32) * MLA_SCALE
        q_pos = i * Q_BLOCK + jnp.arange(Q_BLOCK)
        s = jnp.where(key_pos[None, :] <= q_pos[:, None], s, -jnp.inf)
        p = jax.nn.softmax(s, axis=-1).astype(v.dtype)
        return jnp.einsum('bhqk,bkhd->bqhd', p, v)

    o = lax.map(query_block, (jnp.arange(nb), qn, qr))
    return o.swapaxes(0, 1).reshape(B, S, MLA_HEADS * MLA_D_V)


def mla_sample_attention(q_nope, q_rope, c_kv, k_rope, c_pool, kr_pool, page_table, w_uk, w_uv):
    DB, T = q_nope.shape[0], q_nope.shape[1]
    past = page_table.shape[1] * PAGE_SIZE
    c_past = c_pool[page_table].reshape(DB, past, MLA_KV_LORA)
    kr_past = kr_pool[page_table].reshape(DB, past, MLA_D_ROPE)
    q_lat = jnp.einsum('bthd,chd->bthc', q_nope, w_uk)
    s_past = jnp.einsum('bthc,bpc->bhtp', q_lat, c_past) + jnp.einsum('bthr,bpr->bhtp', q_rope, kr_past)
    s_new = jnp.einsum('bthc,bsc->bhts', q_lat, c_kv) + jnp.einsum('bthr,bsr->bhts', q_rope, k_rope)
    causal = jnp.tril(jnp.ones((T, T), dtype=bool))
    s_new = jnp.where(causal, s_new.astype(jnp.float32), -jnp.inf)
    s = jnp.concatenate([s_past.astype(jnp.float32), s_new], axis=-1) * MLA_SCALE
    p = jax.nn.softmax(s, axis=-1).astype(c_kv.dtype)
    o_lat = jnp.einsum('bhtp,bpc->bthc', p[..., :past], c_past) + jnp.einsum('bhts,bsc->bthc', p[..., past:], c_kv)
    o = jnp.einsum('bthc,chd->bthd', o_lat, w_uv)
    return o.reshape(DB, T, MLA_HEADS * MLA_D_V)


def route_tokens(x2, w_router, router_bias):
    n = x2.shape[0]
    scores = jax.nn.sigmoid(x2.astype(jnp.float32) @ w_router.astype(jnp.float32))
    sel = (scores + router_bias.astype(jnp.float32)).reshape(n, N_GROUPS, EXPERTS_PER_GROUP)
    group_score = lax.top_k(sel, TOP_K)[0].sum(-1)
    _, g_idx = lax.top_k(group_score, 1)
    in_group = jnp.take_along_axis(sel, g_idx[:, :, None], axis=1)[:, 0]
    _, local = lax.top_k(in_group, TOP_K)
    e_idx = g_idx * EXPERTS_PER_GROUP + local
    w = jnp.take_along_axis(scores, e_idx, axis=1)
    w = w / w.sum(-1, keepdims=True)
    return (jax.nn.one_hot(e_idx, N_EXPERTS, dtype=jnp.float32) * w[..., None]).sum(1)


def moe_ffn(x, w_router, router_bias, w_gate, w_up, w_down):
    x2 = x.reshape(-1, x.shape[-1])
    gates = route_tokens(x2, w_router, router_bias)

    def add_expert(acc, expert):
        wg, wu, wd, g = expert
        h = jax.nn.silu(x2 @ wg) * (x2 @ wu)
        return acc + g[:, None] * (h @ wd).astype(jnp.float32), None

    y, _ = lax.scan(add_expert, jnp.zeros(x2.shape, jnp.float32), (w_gate, w_up, w_down, gates.T))
    return y.astype(x.dtype).reshape(x.shape)


def decoder_layer(x, cos, sin, gla_state, gla_chunk_len, attend,
                  w_in, w_gk_up, b_gk, gla_norm_g, q_norm_g, kv_norm_g, w_q_up, w_out,
                  ln1_g, ln1_b, ln2_g, ln2_b, w_router, router_bias, w_gate, w_up, w_down):
    gq, gk, gv, g_lr, g_out, c_q, c_kv, kr_raw = split_projection(x @ w_in)
    o_gla, gla_state_new = gla_mixer(gq, gk, gv, g_lr, g_out, gla_state, gla_chunk_len, w_gk_up, b_gk, gla_norm_g)
    q_nope, q_rope, c_kv_n, k_rope = mla_project(c_q, c_kv, kr_raw, cos, sin, q_norm_g, kv_norm_g, w_q_up)
    o_mla = attend(q_nope, q_rope, c_kv_n, k_rope)
    mixed = jnp.concatenate([o_gla, o_mla], axis=-1) @ w_out
    x = layer_norm(DEEPNORM_ALPHA * x + mixed, ln1_g, ln1_b)
    x = layer_norm(DEEPNORM_ALPHA * x + moe_ffn(x, w_router, router_bias, w_gate, w_up, w_down), ln2_g, ln2_b)
    return x, gla_state_new, c_kv_n, k_rope


def setup_inputs(seed: int = 0) -> dict:
    key = jax.random.key(seed)
    ks = jax.random.split(key, 32)
    n_pages = PAST_LEN // PAGE_SIZE
    n_pool = (DEC_BATCH * n_pages * 5) // 4

    def nrm(k, shape, scale):
        return scale * jax.random.normal(k, shape, jnp.float32)

    def gain(k, shape):
        return 1.0 + 0.02 * jax.random.normal(k, shape, jnp.float32)

    v_lo = 2 * GLA_HEADS * GLA_DK
    v_hi = v_lo + GLA_HEADS * GLA_DV
    w_in = nrm(ks[7], (DEPTH, D_MODEL, D_IN), D_MODEL ** -0.5)
    w_in = w_in.at[:, :, v_lo:v_hi].multiply(DEEPNORM_BETA)
    page_table = jax.random.permutation(ks[6], n_pool)[: DEC_BATCH * n_pages].reshape(DEC_BATCH, n_pages).astype(jnp.int32)
    return {
        'x_prompt': nrm(ks[0], (BATCH, SEQ, D_MODEL), 1.0),
        'x_sample': nrm(ks[1], (DEC_BATCH, DEC_SEQ, D_MODEL), 1.0),
        'cache_kv_latent': nrm(ks[2], (DEPTH, n_pool, PAGE_SIZE, MLA_KV_LORA), 1.0),
        'cache_k_rope': nrm(ks[3], (DEPTH, n_pool, PAGE_SIZE, MLA_D_ROPE), 1.0),
        'state_gla': nrm(ks[4], (DEPTH, DEC_BATCH, GLA_HEADS, GLA_DK, GLA_DV), 0.1),
        'page_table': page_table,
        'w_in': w_in,
        'w_gk_up': nrm(ks[8], (DEPTH, GLA_GATE_RANK, GLA_HEADS * GLA_DK), GLA_GATE_RANK ** -0.5),
        'b_gk': nrm(ks[9], (DEPTH, GLA_HEADS * GLA_DK), 0.1),
        'gla_norm_g': gain(ks[10], (DEPTH, GLA_HEADS * GLA_DV)),
        'q_norm_g': gain(ks[11], (DEPTH, MLA_Q_LORA)),
        'kv_norm_g': gain(ks[12], (DEPTH, MLA_KV_LORA)),
        'w_q_up': nrm(ks[13], (DEPTH, MLA_Q_LORA, MLA_HEADS, MLA_D_NOPE + MLA_D_ROPE), MLA_Q_LORA ** -0.5),
        'w_uk': nrm(ks[14], (DEPTH, MLA_KV_LORA, MLA_HEADS, MLA_D_NOPE), MLA_KV_LORA ** -0.5),
        'w_uv': nrm(ks[15], (DEPTH, MLA_KV_LORA, MLA_HEADS, MLA_D_V), DEEPNORM_BETA * MLA_KV_LORA ** -0.5),
        'w_out': nrm(ks[16], (DEPTH, D_MODEL, D_MODEL), DEEPNORM_BETA * D_MODEL ** -0.5),
        'ln1_g': gain(ks[17], (DEPTH, D_MODEL)),
        'ln1_b': nrm(ks[18], (DEPTH, D_MODEL), 0.02),
        'ln2_g': gain(ks[19], (DEPTH, D_MODEL)),
        'ln2_b': nrm(ks[20], (DEPTH, D_MODEL), 0.02),
        'w_router': nrm(ks[21], (D_MODEL, N_EXPERTS), D_MODEL ** -0.5),
        'router_bias': nrm(ks[22], (N_EXPERTS,), 0.01),
        'w_gate': nrm(ks[23], (DEPTH, N_EXPERTS, D_MODEL, D_FF_EXPERT), D_MODEL ** -0.5),
        'w_up': nrm(ks[24], (DEPTH, N_EXPERTS, D_MODEL, D_FF_EXPERT), D_MODEL ** -0.5),
        'w_down': nrm(ks[25], (DEPTH, N_EXPERTS, D_FF_EXPERT, D_MODEL), DEEPNORM_BETA * D_FF_EXPERT ** -0.5),
    }


def reference(x_prompt, x_sample, cache_kv_latent, cache_k_rope, state_gla, page_table,
              w_in, w_gk_up, b_gk, gla_norm_g, q_norm_g, kv_norm_g, w_q_up, w_uk, w_uv, w_out,
              ln1_g, ln1_b, ln2_g, ln2_b, w_router, router_bias, w_gate, w_up, w_down):
    S = x_prompt.shape[1]
    T = x_sample.shape[1]
    past_len = page_table.shape[1] * PAGE_SIZE
    cos_p, sin_p = rope_cos_sin(jnp.arange(S))
    cos_s, sin_s = rope_cos_sin(past_len + jnp.arange(T))
    zero_state = jnp.zeros((x_prompt.shape[0], GLA_HEADS, GLA_DK, GLA_DV), jnp.float32)
    xp, xs = x_prompt, x_sample
    gla_p, gla_s, lat_p, kr_p, lat_s, kr_s = [], [], [], [], [], []
    for l in range(DEPTH):
        layer_w = (w_in[l], w_gk_up[l], b_gk[l], gla_norm_g[l], q_norm_g[l], kv_norm_g[l], w_q_up[l], w_out[l],
                   ln1_g[l], ln1_b[l], ln2_g[l], ln2_b[l], w_router, router_bias, w_gate[l], w_up[l], w_down[l])
        attend_p = functools.partial(mla_prompt_attention, w_uk=w_uk[l], w_uv=w_uv[l])
        attend_s = functools.partial(mla_sample_attention, c_pool=cache_kv_latent[l], kr_pool=cache_k_rope[l],
                                     page_table=page_table, w_uk=w_uk[l], w_uv=w_uv[l])
        xp, sp, cp, rp = decoder_layer(xp, cos_p, sin_p, zero_state, min(GLA_CHUNK, S), attend_p, *layer_w)
        xs, ss, cs, rs = decoder_layer(xs, cos_s, sin_s, state_gla[l].astype(jnp.float32), T, attend_s, *layer_w)
        gla_p.append(sp.astype(state_gla.dtype))
        gla_s.append(ss.astype(state_gla.dtype))
        lat_p.append(cp)
        kr_p.append(rp)
        lat_s.append(cs)
        kr_s.append(rs)
    return (xp, xs, jnp.stack(gla_p), jnp.stack(gla_s), jnp.stack(lat_p), jnp.stack(kr_p), jnp.stack(lat_s), jnp.stack(kr_s))
```

```python
import functools
import math

import numpy as np
import jax
import jax.numpy as jnp
from jax import lax
from jax.experimental import pallas as pl
from jax.experimental.pallas import tpu as pltpu

F32 = jnp.float32
BF16 = jnp.bfloat16

GLA_GATE_TAU = 16.0
ROPE_THETA = 10000.0
N_GROUPS = 4
TOP_K = 2
LN_EPS = 1e-5
RMS_EPS = 1e-6

LANES = 128
VMEM_LIMIT = 56 * 1024 * 1024

GLA_CHUNK = 64
SAMPLE_PAD = 8
EXPERT_TILE = 256


def _pick(n, prefs):
    for p in prefs:
        if p <= n and n % p == 0:
            return p
    return n


def _cparams(sem, vmem=VMEM_LIMIT):
    return pltpu.CompilerParams(dimension_semantics=sem, vmem_limit_bytes=vmem)


def _dot(a, b):
    return jnp.dot(a, b, preferred_element_type=F32)


def _dot_nt(a, b):
    return lax.dot_general(a, b, (((1,), (1,)), ((), ())), preferred_element_type=F32)


def _dot_tn(a, b):
    return lax.dot_general(a, b, (((0,), (0,)), ((), ())), preferred_element_type=F32)


def _split3(x):
    hi = x.astype(BF16)
    r1 = x - hi.astype(F32)
    mid = r1.astype(BF16)
    lo = (r1 - mid.astype(F32)).astype(BF16)
    return hi, mid, lo


def _silu(x):
    return x * (1.0 / (1.0 + jnp.exp(-x)))


def _layer_norm(y, g, b):
    mu = jnp.mean(y, axis=-1, keepdims=True)
    yc = y - mu
    var = jnp.mean(yc * yc, axis=-1, keepdims=True)
    return yc * lax.rsqrt(var + LN_EPS) * g + b


def _rms_norm(y, g):
    return y * lax.rsqrt(jnp.mean(y * y, axis=-1, keepdims=True) + RMS_EPS) * g


def _mm_slab_kernel(a_ref, b_ref, o_ref, acc_ref):
    k = pl.program_id(2)

    @pl.when(k == 0)
    def _():
        acc_ref[...] = jnp.zeros_like(acc_ref)

    acc_ref[...] += _dot(a_ref[...], b_ref[...])

    @pl.when(k == pl.num_programs(2) - 1)
    def _():
        for j in range(o_ref.shape[0]):
            o_ref[j] = acc_ref[:, j * LANES:(j + 1) * LANES].astype(o_ref.dtype)


def _mm_slab(a, b):
    m, kd = a.shape
    n = b.shape[1]
    tm = _pick(m, (2176, 1088, 1024, 512, 256, 128, 64, 32, 16))
    tn = _pick(n, (1536, 1024, 512, 256, 128))
    tk = _pick(kd, (1024, 512, 256, 128))
    return pl.pallas_call(
        _mm_slab_kernel,
        grid=(n // tn, m // tm, kd // tk),
        in_specs=[pl.BlockSpec((tm, tk), lambda j, i, k: (i, k)),
                  pl.BlockSpec((tk, tn), lambda j, i, k: (k, j))],
        out_specs=pl.BlockSpec((tn // LANES, tm, LANES), lambda j, i, k: (j, i, 0)),
        out_shape=jax.ShapeDtypeStruct((n // LANES, m, LANES), BF16),
        scratch_shapes=[pltpu.VMEM((tm, tn), F32)],
        compiler_params=_cparams(("parallel", "parallel", "arbitrary")),
        name="in_proj_gla",
    )(a, b)


def _mla_in_kernel(a_ref, b_ref, cs_ref, qg_ref, kvg_ref,
                   cq_ref, ckv_ref, kr_ref, glr_ref, acc_ref, *, ql, kvl, dr):
    k = pl.program_id(1)

    @pl.when(k == 0)
    def _():
        acc_ref[...] = jnp.zeros_like(acc_ref)

    acc_ref[...] += _dot(a_ref[...], b_ref[...])

    @pl.when(k == pl.num_programs(1) - 1)
    def _():
        cq = acc_ref[:, 0:ql]
        cq_ref[...] = _rms_norm(cq, qg_ref[...]).astype(cq_ref.dtype)
        ckv = acc_ref[:, ql:ql + kvl]
        ckv_ref[...] = _rms_norm(ckv, kvg_ref[...])
        u = acc_ref[:, ql + kvl:ql + kvl + LANES] * cs_ref[...]
        kr_ref[...] = (u + pltpu.roll(u, dr, 1))[:, 0:dr]
        glr_ref[...] = acc_ref[:, ql + kvl + LANES:ql + kvl + 2 * LANES].astype(glr_ref.dtype)


def _mla_in(xb, wm, cs, qg, kvg, ql, kvl, dr):
    m, kd = xb.shape
    n = wm.shape[1]
    tm = _pick(m, (512, 256, 128, 64, 32, 16))
    tk = _pick(kd, (1024, 512, 256, 128))
    kern = functools.partial(_mla_in_kernel, ql=ql, kvl=kvl, dr=dr)
    return pl.pallas_call(
        kern,
        grid=(m // tm, kd // tk),
        in_specs=[pl.BlockSpec((tm, tk), lambda i, k: (i, k)),
                  pl.BlockSpec((tk, n), lambda i, k: (k, 0)),
                  pl.BlockSpec((tm, LANES), lambda i, k: (i, 0)),
                  pl.BlockSpec((1, ql), lambda i, k: (0, 0)),
                  pl.BlockSpec((1, kvl), lambda i, k: (0, 0))],
        out_specs=[pl.BlockSpec((tm, ql), lambda i, k: (i, 0)),
                   pl.BlockSpec((tm, kvl), lambda i, k: (i, 0)),
                   pl.BlockSpec((tm, dr), lambda i, k: (i, 0)),
                   pl.BlockSpec((tm, LANES), lambda i, k: (i, 0))],
        out_shape=[jax.ShapeDtypeStruct((m, ql), BF16),
                   jax.ShapeDtypeStruct((m, kvl), F32),
                   jax.ShapeDtypeStruct((m, dr), F32),
                   jax.ShapeDtypeStruct((m, LANES), BF16)],
        scratch_shapes=[pltpu.VMEM((tm, n), F32)],
        compiler_params=_cparams(("parallel", "arbitrary")),
        name="in_proj_mla",
    )(xb, wm, cs, qg, kvg)


def _q_up_kernel(cq_ref, w_ref, cs_ref, o_ref, *, dn, dr, scale):
    r = _dot(cq_ref[...], w_ref[...])
    o_ref[:, 0:dn] = (r[:, 0:dn] * scale).astype(o_ref.dtype)
    u = r[:, dn:dn + 2 * dr] * cs_ref[...]
    rot = u + pltpu.roll(u, dr, 1)
    o_ref[:, dn:dn + dr] = (rot[:, 0:dr] * scale).astype(o_ref.dtype)


def _q_up(cq, wq, cs, dn, dr, scale):
    m, ql = cq.shape
    h = wq.shape[0]
    tm = _pick(m, (1088, 1024, 512, 256, 128, 64, 32, 16))
    kern = functools.partial(_q_up_kernel, dn=dn, dr=dr, scale=scale)
    return pl.pallas_call(
        kern,
        grid=(m // tm, h),
        in_specs=[pl.BlockSpec((tm, ql), lambda i, j: (i, 0)),
                  pl.BlockSpec((None, ql, dn + 2 * dr), lambda i, j: (j, 0, 0)),
                  pl.BlockSpec((tm, LANES), lambda i, j: (i, 0))],
        out_specs=pl.BlockSpec((None, tm, dn + dr), lambda i, j: (j, i, 0)),
        out_shape=jax.ShapeDtypeStruct((h, m, dn + dr), BF16),
        compiler_params=_cparams(("parallel", "arbitrary")),
        name="mla_q_up",
    )(cq, wq, cs)


def _kv_up_kernel(c_ref, kr_ref, wk_ref, wv_ref, k_ref, v_ref, *, dn, dr):
    c = c_ref[...].astype(BF16)
    k_ref[:, 0:dn] = _dot(c, wk_ref[...].astype(BF16)).astype(k_ref.dtype)
    k_ref[:, dn:dn + dr] = kr_ref[...].astype(k_ref.dtype)
    v_ref[...] = _dot(c, wv_ref[...].astype(BF16)).astype(v_ref.dtype)


def _kv_up(ckv, kr, w_uk2, w_uv2, n_rows, h, dn, dv, dr):
    kvl = ckv.shape[1]
    tm = _pick(n_rows, (1024, 512, 256, 128, 64, 32, 16))
    kern = functools.partial(_kv_up_kernel, dn=dn, dr=dr)
    return pl.pallas_call(
        kern,
        grid=(n_rows // tm, h),
        in_specs=[pl.BlockSpec((tm, kvl), lambda i, j: (i, 0)),
                  pl.BlockSpec((tm, dr), lambda i, j: (i, 0)),
                  pl.BlockSpec((kvl, dn), lambda i, j: (0, j)),
                  pl.BlockSpec((kvl, dv), lambda i, j: (0, j))],
        out_specs=[pl.BlockSpec((None, tm, dn + dr), lambda i, j: (j, i, 0)),
                   pl.BlockSpec((None, tm, dv), lambda i, j: (j, i, 0))],
        out_shape=[jax.ShapeDtypeStruct((h, n_rows, dn + dr), BF16),
                   jax.ShapeDtypeStruct((h, n_rows, dv), BF16)],
        compiler_params=_cparams(("parallel", "arbitrary")),
        name="mla_kv_up",
    )(ckv, kr, w_uk2, w_uv2)


def _attn_prompt_kernel(q_ref, k_ref, v_ref, o_ref, *, tq):
    s_len = q_ref.shape[0]
    for i in range(s_len // tq):
        hi = (i + 1) * tq
        q = q_ref[i * tq:hi, :]
        s = _dot_nt(q, k_ref[0:hi, :])
        row = lax.broadcasted_iota(jnp.int32, s.shape, 0) + i * tq
        col = lax.broadcasted_iota(jnp.int32, s.shape, 1)
        s = jnp.where(col <= row, s, -jnp.inf)
        m = jnp.max(s, axis=-1, keepdims=True)
        p = jnp.exp(s - m)
        l = jnp.sum(p, axis=-1, keepdims=True)
        o = _dot(p.astype(BF16), v_ref[0:hi, :])
        o_ref[i * tq:hi, :] = (o / l).astype(o_ref.dtype)


def _attn_prompt(q3, k3, v3, batch, s_len):
    h, _, dqk = q3.shape
    dv = v3.shape[2]
    tq = _pick(s_len, (256, 128, 64, 32, 16))
    kern = functools.partial(_attn_prompt_kernel, tq=tq)
    return pl.pallas_call(
        kern,
        grid=(batch, h),
        in_specs=[pl.BlockSpec((None, s_len, dqk), lambda b, j: (j, b, 0)),
                  pl.BlockSpec((None, s_len, dqk), lambda b, j: (j, b, 0)),
                  pl.BlockSpec((None, s_len, dv), lambda b, j: (j, b, 0))],
        out_specs=pl.BlockSpec((None, s_len, dv), lambda b, j: (j, b, 0)),
        out_shape=jax.ShapeDtypeStruct((h, batch * s_len, dv), BF16),
        compiler_params=_cparams(("parallel", "parallel")),
        name="mla_attn_prompt",
    )(q3, k3, v3)


def _q_lat_kernel(q_ref, w_ref, o_ref, *, dn):
    o_ref[...] = _dot_nt(q_ref[:, 0:dn], w_ref[...].astype(BF16)).astype(o_ref.dtype)


def _q_lat(q3, w_uk2, row_block, n_rows, dn):
    h, _, dqk = q3.shape
    kvl = w_uk2.shape[0]
    kern = functools.partial(_q_lat_kernel, dn=dn)
    return pl.pallas_call(
        kern,
        grid=(h,),
        in_specs=[pl.BlockSpec((None, n_rows, dqk), lambda j: (j, row_block, 0)),
                  pl.BlockSpec((kvl, dn), lambda j: (0, j))],
        out_specs=pl.BlockSpec((None, n_rows, kvl), lambda j: (j, 0, 0)),
        out_shape=jax.ShapeDtypeStruct((h, n_rows, kvl), BF16),
        compiler_params=_cparams(("parallel",)),
        name="mla_q_absorb",
    )(q3, w_uk2)


def _o_uv_kernel(o_ref, w_ref, y_ref):
    y_ref[...] = _dot(o_ref[...], w_ref[...].astype(BF16)).astype(y_ref.dtype)


def _o_uv(o_lat3, w_uv2, dv):
    h, n_rows, kvl = o_lat3.shape
    return pl.pallas_call(
        _o_uv_kernel,
        grid=(h,),
        in_specs=[pl.BlockSpec((None, n_rows, kvl), lambda j: (j, 0, 0)),
                  pl.BlockSpec((kvl, dv), lambda j: (0, j))],
        out_specs=pl.BlockSpec((None, n_rows, dv), lambda j: (j, 0, 0)),
        out_shape=jax.ShapeDtypeStruct((h, n_rows, dv), BF16),
        compiler_params=_cparams(("parallel",)),
        name="mla_o_absorb",
    )(o_lat3, w_uv2)


def _attn_sample_kernel(pt_ref, ql_ref, qr_ref, cn_ref, kn_ref, cc_hbm, kc_hbm, o_ref,
                        cbuf, kbuf, sem, *, layer, n_pages, page, t_real):
    b = pl.program_id(0)
    nb = pl.num_programs(0)
    slot = b % 2

    def copies(bb, sl, p):
        pg = pt_ref[bb * n_pages + p]
        dst = pl.ds(p * page, page)
        return (pltpu.make_async_copy(cc_hbm.at[layer, pg], cbuf.at[sl, dst], sem.at[0, sl]),
                pltpu.make_async_copy(kc_hbm.at[layer, pg], kbuf.at[sl, dst], sem.at[1, sl]))

    def start(bb, sl):
        for p in range(n_pages):
            for c in copies(bb, sl, p):
                c.start()

    @pl.when(b == 0)
    def _():
        start(0, 0)

    @pl.when(b + 1 < nb)
    def _():
        start(b + 1, 1 - slot)

    for p in range(n_pages):
        for c in copies(b, slot, p):
            c.wait()

    cb = cbuf[slot].astype(BF16)
    kb = kbuf[slot].astype(BF16)
    ql = ql_ref[...]
    qr = qr_ref[...]
    cn = cn_ref[...].astype(BF16)
    kn = kn_ref[...].astype(BF16)
    s = _dot_nt(ql, cb) + _dot_nt(qr, kb)
    sn = _dot_nt(ql, cn) + _dot_nt(qr, kn)
    t_of_row = lax.broadcasted_iota(jnp.int32, sn.shape, 0) % t_real
    j = lax.broadcasted_iota(jnp.int32, sn.shape, 1)
    sn = jnp.where(j <= t_of_row, sn, -jnp.inf)
    m = jnp.maximum(jnp.max(s, axis=-1, keepdims=True), jnp.max(sn, axis=-1, keepdims=True))
    p = jnp.exp(s - m)
    pn = jnp.exp(sn - m)
    l = jnp.sum(p, axis=-1, keepdims=True) + jnp.sum(pn, axis=-1, keepdims=True)
    o = _dot(p.astype(BF16), cb) + _dot(pn.astype(BF16), cn)
    o_ref[...] = (o / l).astype(o_ref.dtype)


def _attn_sample(page_table, qlat, qrope, c_new, k_new, cache_c, cache_k, layer, t_real):
    db, ht, kvl = qlat.shape
    dr = qrope.shape[2]
    n_pages = page_table.shape[1]
    page = cache_c.shape[2]
    past = n_pages * page
    tn = c_new.shape[1]
    kern = functools.partial(_attn_sample_kernel, layer=layer, n_pages=n_pages, page=page,
                             t_real=t_real)
    grid_spec = pltpu.PrefetchScalarGridSpec(
        num_scalar_prefetch=1,
        grid=(db,),
        in_specs=[pl.BlockSpec((None, ht, kvl), lambda b, pt: (b, 0, 0)),
                  pl.BlockSpec((None, ht, dr), lambda b, pt: (b, 0, 0)),
                  pl.BlockSpec((None, tn, kvl), lambda b, pt: (b, 0, 0)),
                  pl.BlockSpec((None, tn, dr), lambda b, pt: (b, 0, 0)),
                  pl.BlockSpec(memory_space=pl.ANY),
                  pl.BlockSpec(memory_space=pl.ANY)],
        out_specs=pl.BlockSpec((None, ht, kvl), lambda b, pt: (b, 0, 0)),
        scratch_shapes=[pltpu.VMEM((2, past, kvl), F32),
                        pltpu.VMEM((2, past, dr), F32),
                        pltpu.SemaphoreType.DMA((2, 2))],
    )
    return pl.pallas_call(
        kern,
        grid_spec=grid_spec,
        out_shape=jax.ShapeDtypeStruct((db, ht, kvl), BF16),
        compiler_params=_cparams(("arbitrary",)),
        name="mla_attn_sample",
    )(page_table.reshape(-1), qlat, qrope, c_new, k_new, cache_c, cache_k)


def _gla_constants(c, g, dv):
    nlev = int(math.log2(g))
    ng = c // g
    r = np.arange(c)
    t = r[:, None]
    j = r[None, :]
    same = (t // g) == (j // g)
    d = [same & (j <= t), same & (j > t)]
    m = [np.eye(c, dtype=bool)]
    for lvl in range(1, nlev + 1):
        size = 1 << lvl
        half = size >> 1
        blk = r // size
        mid = (blk * size + half - 1)[:, None]
        upper = ((r % size) >= half)[:, None]
        d.append(np.where(upper, (j > mid) & (j <= t), (j > t) & (j <= mid)))
        m.append((blk[:, None] == blk[None, :]) & upper & ~upper.T)
    col_group = np.arange(ng * dv)[None, :] // dv
    sel = (t == col_group * g + g - 1)
    bd = (t // g) == col_group
    return (jnp.asarray(np.concatenate(d, 0), BF16), jnp.asarray(np.stack(m), F32),
            jnp.asarray(sel, BF16), jnp.asarray(bd, BF16))


def _gla_chunk(q, k, v, go, glr, wgk, bgk, gn, dmat, m_ref, sel, bd, s_cat, *, c, g, t_real, dk):
    ng = c // g
    nlev = int(math.log2(g))
    dv = v.shape[1]
    x = _dot(glr, wgk) + bgk
    lg = (jnp.minimum(x, 0.0) - jnp.log1p(jnp.exp(-jnp.abs(x)))) * (1.0 / GLA_GATE_TAU)
    if t_real < g:
        rr = lax.broadcasted_iota(jnp.int32, lg.shape, 0) % g
        lg = jnp.where(rr < t_real, lg, 0.0)
    l3 = _split3(lg)
    ex = jnp.exp(_dot(dmat, l3[0]) + _dot(dmat, l3[1]) + _dot(dmat, l3[2]))
    eb = ex[0:c]
    ekd = ex[c:2 * c]
    qf = q.astype(F32) * (dk ** -0.5)
    kf = k.astype(F32)
    attn = m_ref[0] * _dot_nt(qf.astype(BF16), k)
    for lvl in range(1, nlev + 1):
        e = ex[(1 + lvl) * c:(2 + lvl) * c]
        attn = attn + m_ref[lvl] * _dot_nt((qf * e).astype(BF16), (kf * e).astype(BF16))
    o = _dot(attn.astype(BF16), v)
    oi = _dot((qf * eb).astype(BF16), s_cat.astype(BF16))
    if ng == 1:
        o = o + oi
        vd = v
    else:
        o = o + jnp.concatenate(
            [oi[gi * g:(gi + 1) * g, gi * dv:(gi + 1) * dv] for gi in range(ng)], axis=0)
        vd = jnp.concatenate([v] * ng, axis=1) * bd
    e3 = _split3(eb)
    dec = _dot_tn(e3[0], sel) + _dot_tn(e3[1], sel) + _dot_tn(e3[2], sel)
    s_new = s_cat * dec + _dot_tn((kf * ekd).astype(BF16), vd)
    out = _rms_norm(o, gn) * _silu(go.astype(F32))
    return out, s_new


def _slabs(ref, rows):
    return jnp.concatenate([ref[j, rows, :] for j in range(ref.shape[0])], axis=1)


def _gla_prompt_kernel(q_ref, k_ref, v_ref, go_ref, glr_ref, wgk_ref, bgk_ref, gn_ref,
                       d_ref, m_ref, sel_ref, bd_ref, o_ref, s_ref, st_ref, *, c, dk):
    @pl.when(pl.program_id(2) == 0)
    def _():
        st_ref[...] = jnp.zeros_like(st_ref)

    def body(i, carry):
        rows = pl.ds(pl.multiple_of(i * c, c), c)
        out, s_new = _gla_chunk(
            q_ref[rows, :], k_ref[rows, :], _slabs(v_ref, rows), _slabs(go_ref, rows),
            glr_ref[rows, :], wgk_ref[...], bgk_ref[...], gn_ref[...], d_ref[...], m_ref,
            sel_ref[...], bd_ref[...], st_ref[...], c=c, g=c, t_real=c, dk=dk)
        st_ref[...] = s_new
        for j in range(o_ref.shape[0]):
            o_ref[j, rows, :] = out[:, j * LANES:(j + 1) * LANES].astype(o_ref.dtype)
        return carry

    lax.fori_loop(0, q_ref.shape[0] // c, body, 0)
    s_ref[...] = st_ref[...]


def _gla_prompt(g3, glr, wgk, bgk, gn, consts, batch, s_len, hg, dk, dv):
    c = GLA_CHUNK
    nv = dv // LANES
    tb = _pick(s_len, (512, 256, 128, 64))
    nb = s_len // tb
    dmat, mmat, sel, bd = consts
    kern = functools.partial(_gla_prompt_kernel, c=c, dk=dk)
    rowmap = lambda off: (lambda b, h, t: (off + h, b * nb + t, 0))
    const2 = lambda b, h, t: (0, 0)
    return pl.pallas_call(
        kern,
        grid=(batch, hg, nb),
        in_specs=[pl.BlockSpec((None, tb, LANES), rowmap(0)),
                  pl.BlockSpec((None, tb, LANES), rowmap(hg)),
                  pl.BlockSpec((nv, tb, LANES), rowmap(2 * hg // nv)),
                  pl.BlockSpec((nv, tb, LANES), rowmap(2 * hg // nv + hg)),
                  pl.BlockSpec((tb, LANES), lambda b, h, t: (b * nb + t, 0)),
                  pl.BlockSpec((LANES, dk), lambda b, h, t: (0, h)),
                  pl.BlockSpec((1, dk), lambda b, h, t: (0, h)),
                  pl.BlockSpec((1, dv), lambda b, h, t: (0, h)),
                  pl.BlockSpec(dmat.shape, const2),
                  pl.BlockSpec(mmat.shape, lambda b, h, t: (0, 0, 0)),
                  pl.BlockSpec(sel.shape, const2),
                  pl.BlockSpec(bd.shape, const2)],
        out_specs=[pl.BlockSpec((nv, tb, LANES), lambda b, h, t: (h, b * nb + t, 0)),
                   pl.BlockSpec((None, None, dk, dv), lambda b, h, t: (b, h, 0, 0))],
        out_shape=[jax.ShapeDtypeStruct((hg * nv, batch * s_len, LANES), BF16),
                   jax.ShapeDtypeStruct((batch, hg, dk, dv), F32)],
        scratch_shapes=[pltpu.VMEM((dk, dv), F32)],
        compiler_params=_cparams(("parallel", "parallel", "arbitrary")),
        name="gla_prompt",
    )(g3, g3, g3, g3, glr, wgk, bgk, gn, dmat, mmat, sel, bd)


def _gla_sample_kernel(q_ref, k_ref, v_ref, go_ref, glr_ref, wgk_ref, bgk_ref, gn_ref,
                       d_ref, m_ref, sel_ref, bd_ref, s0_ref, o_ref, s_ref, *, c, g, t_real, dk):
    ng = c // g
    dv = s0_ref.shape[2]
    rows = slice(None)
    s_cat = jnp.concatenate([s0_ref[gi] for gi in range(ng)], axis=1)
    out, s_new = _gla_chunk(
        q_ref[...], k_ref[...], _slabs(v_ref, rows), _slabs(go_ref, rows), glr_ref[...],
        wgk_ref[...], bgk_ref[...], gn_ref[...], d_ref[...], m_ref, sel_ref[...], bd_ref[...],
        s_cat, c=c, g=g, t_real=t_real, dk=dk)
    for j in range(o_ref.shape[0]):
        o_ref[j] = out[:, j * LANES:(j + 1) * LANES].astype(o_ref.dtype)
    for gi in range(ng):
        s_ref[gi] = s_new[:, gi * dv:(gi + 1) * dv]


def _gla_sample(g3s, glr_s, wgk, bgk, gn, consts, state, layer, dec_batch, t_real, hg, dk, dv):
    c = GLA_CHUNK
    g = SAMPLE_PAD
    ng = c // g
    nv = dv // LANES
    dmat, mmat, sel, bd = consts
    kern = functools.partial(_gla_sample_kernel, c=c, g=g, t_real=t_real, dk=dk)
    rowmap = lambda off: (lambda i, h: (off + h, i, 0))
    const2 = lambda i, h: (0, 0)
    return pl.pallas_call(
        kern,
        grid=(dec_batch // ng, hg),
        in_specs=[pl.BlockSpec((None, c, LANES), rowmap(0)),
                  pl.BlockSpec((None, c, LANES), rowmap(hg)),
                  pl.BlockSpec((nv, c, LANES), rowmap(2 * hg // nv)),
                  pl.BlockSpec((nv, c, LANES), rowmap(2 * hg // nv + hg)),
                  pl.BlockSpec((c, LANES), lambda i, h: (i, 0)),
                  pl.BlockSpec((LANES, dk), lambda i, h: (0, h)),
                  pl.BlockSpec((1, dk), lambda i, h: (0, h)),
                  pl.BlockSpec((1, dv), lambda i, h: (0, h)),
                  pl.BlockSpec(dmat.shape, const2),
                  pl.BlockSpec(mmat.shape, lambda i, h: (0, 0, 0)),
                  pl.BlockSpec(sel.shape, const2),
                  pl.BlockSpec(bd.shape, const2),
                  pl.BlockSpec((None, ng, None, dk, dv), lambda i, h: (layer, i, h, 0, 0))],
        out_specs=[pl.BlockSpec((nv, c, LANES), lambda i, h: (h, i, 0)),
                   pl.BlockSpec((ng, None, dk, dv), lambda i, h: (i, h, 0, 0))],
        out_shape=[jax.ShapeDtypeStruct((hg * nv, dec_batch * g, LANES), BF16),
                   jax.ShapeDtypeStruct((dec_batch, hg, dk, dv), F32)],
        compiler_params=_cparams(("parallel", "parallel")),
        name="gla_sample",
    )(g3s, g3s, g3s, g3s, glr_s, wgk, bgk, gn, dmat, mmat, sel, bd, state)


def _out_ln_kernel(a_ref, b_ref, x_ref, g_ref, bb_ref, of_ref, ob_ref, acc_ref, *, alpha):
    k = pl.program_id(1)

    @pl.when(k == 0)
    def _():
        acc_ref[...] = jnp.zeros_like(acc_ref)

    a = jnp.concatenate([a_ref[j] for j in range(a_ref.shape[0])], axis=1)
    acc_ref[...] += _dot(a, b_ref[...])

    @pl.when(k == pl.num_programs(1) - 1)
    def _():
        z = _layer_norm(alpha * x_ref[...] + acc_ref[...], g_ref[...], bb_ref[...])
        of_ref[...] = z
        ob_ref[...] = z.astype(ob_ref.dtype)


def _out_ln(mixed3, w_out, x, g, b, alpha):
    ns, m, _ = mixed3.shape
    d = w_out.shape[1]
    tm = _pick(m, (256, 128, 64, 32, 16))
    tk = _pick(ns * LANES, (512, 256, 128))
    kern = functools.partial(_out_ln_kernel, alpha=alpha)
    return pl.pallas_call(
        kern,
        grid=(m // tm, ns * LANES // tk),
        in_specs=[pl.BlockSpec((tk // LANES, tm, LANES), lambda i, k: (k, i, 0)),
                  pl.BlockSpec((tk, d), lambda i, k: (k, 0)),
                  pl.BlockSpec((tm, d), lambda i, k: (i, 0)),
                  pl.BlockSpec((1, d), lambda i, k: (0, 0)),
                  pl.BlockSpec((1, d), lambda i, k: (0, 0))],
        out_specs=[pl.BlockSpec((tm, d), lambda i, k: (i, 0)),
                   pl.BlockSpec((tm, d), lambda i, k: (i, 0))],
        out_shape=[jax.ShapeDtypeStruct((m, d), F32), jax.ShapeDtypeStruct((m, d), BF16)],
        scratch_shapes=[pltpu.VMEM((tm, d), F32)],
        compiler_params=_cparams(("parallel", "arbitrary")),
        name="out_proj_ln",
    )(mixed3, w_out, x, g, b)


def _first_max(vals):
    best = vals[0]
    idx = jnp.zeros(best.shape, jnp.int32)
    for j in range(1, len(vals)):
        upd = vals[j] > best
        idx = jnp.where(upd, j, idx)
        best = jnp.where(upd, vals[j], best)
    return best, idx


def _pick_row(vals, idx):
    out = vals[0]
    for j in range(1, len(vals)):
        out = jnp.where(idx == j, vals[j], out)
    return out


def _router_kernel(x_ref, wr_ref, bias_ref, tri_ref, eidx_ref, w_ref, rank_ref, cnt_ref,
                   run_ref, *, n_exp, n_groups):
    @pl.when(pl.program_id(0) == 0)
    def _():
        run_ref[...] = jnp.zeros_like(run_ref)

    epg = n_exp // n_groups
    logits = lax.dot_general(wr_ref[...], x_ref[...], (((1,), (1,)), ((), ())),
                             precision=lax.Precision.HIGHEST, preferred_element_type=F32)
    sc = 1.0 / (1.0 + jnp.exp(-logits))
    sel = sc + bias_ref[:, 0:1]
    sel_rows = [sel[e:e + 1, :] for e in range(n_exp)]
    sc_rows = [sc[e:e + 1, :] for e in range(n_exp)]
    group_scores = []
    for gi in range(n_groups):
        r = sel_rows[gi * epg:(gi + 1) * epg]
        best = None
        for a in range(epg):
            for b in range(a + 1, epg):
                s = r[a] + r[b]
                best = s if best is None else jnp.maximum(best, s)
        group_scores.append(best)
    _, g_idx = _first_max(group_scores)
    in_sel = [_pick_row([sel_rows[gi * epg + j] for gi in range(n_groups)], g_idx)
              for j in range(epg)]
    in_sc = [_pick_row([sc_rows[gi * epg + j] for gi in range(n_groups)], g_idx)
             for j in range(epg)]
    _, l1 = _first_max(in_sel)
    _, l2 = _first_max([jnp.where(l1 == j, -jnp.inf, in_sel[j]) for j in range(epg)])
    w1 = _pick_row(in_sc, l1)
    w2 = _pick_row(in_sc, l2)
    den = w1 + w2
    e1 = g_idx * epg + l1
    e2 = g_idx * epg + l2
    eiota = lax.broadcasted_iota(jnp.int32, sc.shape, 0)
    oh1 = (eiota == e1).astype(F32)
    oh2 = (eiota == e2).astype(F32)
    oh = oh1 + oh2
    before = run_ref[:, 0:1] + _dot(oh.astype(BF16), tri_ref[...])
    eidx_ref[0:1, :] = e1
    eidx_ref[1:2, :] = e2
    w_ref[0:1, :] = w1 / den
    w_ref[1:2, :] = w2 / den
    rank_ref[0:1, :] = jnp.sum(oh1 * before, axis=0, keepdims=True).astype(jnp.int32)
    rank_ref[1:2, :] = jnp.sum(oh2 * before, axis=0, keepdims=True).astype(jnp.int32)
    run_ref[...] += jnp.sum(oh, axis=1, keepdims=True)
    cnt_ref[...] = run_ref[...].astype(jnp.int32)


def _router(x, wr_t, bias, n_groups):
    m, d = x.shape
    n_exp = wr_t.shape[0]
    tm = _pick(m, (512, 256, 128))
    tri = jnp.asarray(np.triu(np.ones((tm, tm), np.float32), 1), BF16)
    kern = functools.partial(_router_kernel, n_exp=n_exp, n_groups=n_groups)
    return pl.pallas_call(
        kern,
        grid=(m // tm,),
        in_specs=[pl.BlockSpec((tm, d), lambda i: (i, 0)),
                  pl.BlockSpec((n_exp, d), lambda i: (0, 0)),
                  pl.BlockSpec((n_exp, LANES), lambda i: (0, 0)),
                  pl.BlockSpec((tm, tm), lambda i: (0, 0))],
        out_specs=[pl.BlockSpec((TOP_K, tm), lambda i: (0, i)),
                   pl.BlockSpec((TOP_K, tm), lambda i: (0, i)),
                   pl.BlockSpec((TOP_K, tm), lambda i: (0, i)),
                   pl.BlockSpec((n_exp, LANES), lambda i: (0, 0))],
        out_shape=[jax.ShapeDtypeStruct((TOP_K, m), jnp.int32),
                   jax.ShapeDtypeStruct((TOP_K, m), F32),
                   jax.ShapeDtypeStruct((TOP_K, m), jnp.int32),
                   jax.ShapeDtypeStruct((n_exp, LANES), jnp.int32)],
        scratch_shapes=[pltpu.VMEM((n_exp, LANES), F32)],
        compiler_params=_cparams(("arbitrary",)),
        name="moe_router",
    )(x, wr_t, bias, tri)


def _route_plan(eidx, rank, cnt, tile, n_tok):
    n_exp = cnt.shape[0]
    counts = cnt[:, 0]
    padded = ((counts + tile - 1) // tile) * tile
    ends = jnp.cumsum(padded)
    offs = ends - padded
    pos = (offs[eidx] + rank).astype(jnp.int32)
    t_max = (TOP_K * n_tok) // tile + n_exp
    tiles = jnp.arange(t_max, dtype=jnp.int32)
    n_valid = (ends[-1] // tile).astype(jnp.int32)
    valid = tiles < n_valid
    row_block = jnp.where(valid, tiles, n_valid - 1)
    tile_exp = jnp.minimum(jnp.searchsorted(ends, row_block * tile, side="right"),
                           n_exp - 1).astype(jnp.int32)
    first = (valid & (tiles * tile == offs[tile_exp])).astype(jnp.int32)
    zero_row = jnp.where(padded > 0, ends - tile, -1).astype(jnp.int32)
    return pos.reshape(-1), tile_exp, row_block, first, n_valid.reshape(1), zero_row, t_max


def _dispatch_kernel(pos_ref, zrow_ref, x_hbm, xs_hbm, zbuf, sem, zsem, *, tm, n_tok, tile, n_exp):
    i = pl.program_id(0)

    def zero_copy(e):
        r0 = pl.multiple_of(jnp.maximum(zrow_ref[e], 0), tile)
        return pltpu.make_async_copy(zbuf, xs_hbm.at[pl.ds(r0, tile)], zsem)

    @pl.when(i == 0)
    def _():
        zbuf[...] = jnp.zeros_like(zbuf)
        for e in range(n_exp):
            @pl.when(zrow_ref[e] >= 0)
            def _():
                zero_copy(e).start()
        for e in range(n_exp):
            @pl.when(zrow_ref[e] >= 0)
            def _():
                zero_copy(e).wait()

    def row_copy(n, s):
        p = pos_ref[s * n_tok + n]
        return pltpu.make_async_copy(x_hbm.at[pl.ds(n, 1)], xs_hbm.at[pl.ds(p, 1)], sem)

    def issue(r, carry):
        for s in range(TOP_K):
            row_copy(i * tm + r, s).start()
        return carry

    def drain(r, carry):
        for s in range(TOP_K):
            row_copy(i * tm + r, s).wait()
        return carry

    lax.fori_loop(0, tm, issue, 0)
    lax.fori_loop(0, tm, drain, 0)


def _dispatch(pos, zero_row, x, tile, n_rows):
    n_tok, d = x.shape
    n_exp = zero_row.shape[0]
    tm = _pick(n_tok, (512, 256, 128, 64, 32, 16))
    kern = functools.partial(_dispatch_kernel, tm=tm, n_tok=n_tok, tile=tile, n_exp=n_exp)
    grid_spec = pltpu.PrefetchScalarGridSpec(
        num_scalar_prefetch=2,
        grid=(n_tok // tm,),
        in_specs=[pl.BlockSpec(memory_space=pl.ANY)],
        out_specs=pl.BlockSpec(memory_space=pl.ANY),
        scratch_shapes=[pltpu.VMEM((tile, d), x.dtype),
                        pltpu.SemaphoreType.DMA(()),
                        pltpu.SemaphoreType.DMA(())],
    )
    return pl.pallas_call(
        kern,
        grid_spec=grid_spec,
        out_shape=jax.ShapeDtypeStruct((n_rows, d), x.dtype),
        compiler_params=_cparams(("arbitrary",)),
        name="moe_dispatch",
    )(pos, zero_row, x)


def _expert_up_kernel(te_ref, rb_ref, first_ref, nv_ref, x_ref, wg_ref, wu_ref, h_ref, wgb, wub):
    t = pl.program_id(1)

    @pl.when(t < nv_ref[0])
    def _():
        @pl.when(first_ref[t] == 1)
        def _():
            wgb[...] = wg_ref[...].astype(BF16)
            wub[...] = wu_ref[...].astype(BF16)

        xb = x_ref[...].astype(BF16)
        h_ref[...] = (_silu(_dot(xb, wgb[...])) * _dot(xb, wub[...])).astype(h_ref.dtype)


def _expert_up(plan, xs, w_gate, w_up, layer, tile):
    _, tile_exp, row_block, first, n_valid, _, t_max = plan
    n_rows, d = xs.shape
    f = w_gate.shape[3]
    tf = _pick(f, (512, 256, 128))
    wmap = lambda j, t, te, rb, fi, nv: (layer, te[t], 0, j)
    grid_spec = pltpu.PrefetchScalarGridSpec(
        num_scalar_prefetch=4,
        grid=(f // tf, t_max),
        in_specs=[pl.BlockSpec((tile, d), lambda j, t, te, rb, fi, nv: (rb[t], 0)),
                  pl.BlockSpec((None, None, d, tf), wmap),
                  pl.BlockSpec((None, None, d, tf), wmap)],
        out_specs=pl.BlockSpec((tile, tf), lambda j, t, te, rb, fi, nv: (rb[t], j)),
        scratch_shapes=[pltpu.VMEM((d, tf), BF16), pltpu.VMEM((d, tf), BF16)],
    )
    return pl.pallas_call(
        _expert_up_kernel,
        grid_spec=grid_spec,
        out_shape=jax.ShapeDtypeStruct((n_rows, f), BF16),
        compiler_params=_cparams(("arbitrary", "arbitrary"), 60 * 1024 * 1024),
        name="moe_expert_up",
    )(tile_exp, row_block, first, n_valid, xs, w_gate, w_up)


def _expert_down_kernel(te_ref, rb_ref, first_ref, nv_ref, h_ref, wd_ref, y_ref, wdb):
    t = pl.program_id(1)

    @pl.when(t < nv_ref[0])
    def _():
        @pl.when(first_ref[t] == 1)
        def _():
            wdb[...] = wd_ref[...].astype(BF16)

        y_ref[...] = _dot(h_ref[...], wdb[...])


def _expert_down(plan, hs, w_down, layer, tile):
    _, tile_exp, row_block, first, n_valid, _, t_max = plan
    n_rows, f = hs.shape
    d = w_down.shape[3]
    tn = _pick(d, (2048, 1024, 512, 256, 128))
    grid_spec = pltpu.PrefetchScalarGridSpec(
        num_scalar_prefetch=4,
        grid=(d // tn, t_max),
        in_specs=[pl.BlockSpec((tile, f), lambda j, t, te, rb, fi, nv: (rb[t], 0)),
                  pl.BlockSpec((None, None, f, tn), lambda j, t, te, rb, fi, nv: (layer, te[t], 0, j))],
        out_specs=pl.BlockSpec((tile, tn), lambda j, t, te, rb, fi, nv: (rb[t], j)),
        scratch_shapes=[pltpu.VMEM((f, tn), BF16)],
    )
    return pl.pallas_call(
        _expert_down_kernel,
        grid_spec=grid_spec,
        out_shape=jax.ShapeDtypeStruct((n_rows, d), F32),
        compiler_params=_cparams(("arbitrary", "arbitrary")),
        name="moe_expert_down",
    )(tile_exp, row_block, first, n_valid, hs, w_down)


def _combine_ln_kernel(pos_ref, x_ref, w_ref, g_ref, b_ref, ys_hbm, of_ref, ob_ref, buf, sem,
                       *, tm, n_tok, alpha):
    i = pl.program_id(0)

    def row_copy(r, s):
        p = pos_ref[s * n_tok + i * tm + r]
        return pltpu.make_async_copy(ys_hbm.at[pl.ds(p, 1)], buf.at[s, pl.ds(r, 1)], sem.at[s])

    def issue(r, carry):
        for s in range(TOP_K):
            row_copy(r, s).start()
        return carry

    def drain(r, carry):
        for s in range(TOP_K):
            row_copy(r, s).wait()
        return carry

    lax.fori_loop(0, tm, issue, 0)
    lax.fori_loop(0, tm, drain, 0)
    y = w_ref[:, 0:1] * buf[0] + w_ref[:, 1:2] * buf[1]
    z = _layer_norm(alpha * x_ref[...] + y, g_ref[...], b_ref[...])
    of_ref[...] = z
    ob_ref[...] = z.astype(ob_ref.dtype)


def _combine_ln(pos, x, wts, g, b, ys, alpha):
    m, d = x.shape
    tm = _pick(m, (256, 128, 64, 32, 16))
    kern = functools.partial(_combine_ln_kernel, tm=tm, n_tok=m, alpha=alpha)
    grid_spec = pltpu.PrefetchScalarGridSpec(
        num_scalar_prefetch=1,
        grid=(m // tm,),
        in_specs=[pl.BlockSpec((tm, d), lambda i, p: (i, 0)),
                  pl.BlockSpec((tm, TOP_K), lambda i, p: (i, 0)),
                  pl.BlockSpec((1, d), lambda i, p: (0, 0)),
                  pl.BlockSpec((1, d), lambda i, p: (0, 0)),
                  pl.BlockSpec(memory_space=pl.ANY)],
        out_specs=[pl.BlockSpec((tm, d), lambda i, p: (i, 0)),
                   pl.BlockSpec((tm, d), lambda i, p: (i, 0))],
        scratch_shapes=[pltpu.VMEM((TOP_K, tm, d), F32),
                        pltpu.SemaphoreType.DMA((TOP_K,))],
    )
    return pl.pallas_call(
        kern,
        grid_spec=grid_spec,
        out_shape=[jax.ShapeDtypeStruct((m, d), F32), jax.ShapeDtypeStruct((m, d), BF16)],
        compiler_params=_cparams(("arbitrary",)),
        name="moe_combine_ln",
    )(pos, x, wts, g, b, ys)


def _rope_table(pos, dr):
    half = dr // 2
    inv_freq = ROPE_THETA ** (-(jnp.arange(half, dtype=F32) * 2.0) / dr)
    ang = pos.astype(F32)[:, None] * inv_freq[None, :]
    cos, sin = jnp.cos(ang), jnp.sin(ang)
    return jnp.concatenate([cos, cos, sin, sin], axis=-1)


def _rotate_half_cols(w, dr):
    half = dr // 2
    return jnp.concatenate([-w[..., half:], w[..., :half]], axis=-1)


def kernel(x_prompt, x_sample, cache_kv_latent, cache_k_rope, state_gla, page_table, w_in, w_gk_up, b_gk, gla_norm_g, q_norm_g, kv_norm_g, w_q_up, w_uk, w_uv, w_out, ln1_g, ln1_b, ln2_g, ln2_b, w_router, router_bias, w_gate, w_up, w_down):
    batch, s_len, d = x_prompt.shape
    dec_batch, t_len, _ = x_sample.shape
    depth = w_in.shape[0]
    _, _, hg, dk, dv = state_gla.shape
    rank = w_gk_up.shape[1]
    ql = q_norm_g.shape[1]
    kvl = kv_norm_g.shape[1]
    hm, dn = w_uk.shape[2], w_uk.shape[3]
    dvm = w_uv.shape[3]
    dr = w_q_up.shape[3] - dn
    n_exp = w_router.shape[1]
    page = cache_kv_latent.shape[2]
    past_len = page_table.shape[1] * page
    n_p = batch * s_len
    n_s = dec_batch * t_len
    n_tok = n_p + n_s
    assert dk == LANES and dv % LANES == 0 and 2 * dr == LANES and dn == LANES and dvm == LANES
    assert rank <= LANES and n_p % n_s == 0 and t_len <= SAMPLE_PAD
    alpha = (2.0 * depth) ** 0.25
    mla_scale = (dn + dr) ** -0.5
    hk, hv = hg * dk, hg * dv
    nv = dv // LANES

    o_glr = 2 * hk + hv
    o_gout = o_glr + rank
    o_cq = o_gout + hv
    o_kr = o_cq + ql + kvl
    w_gla = jnp.concatenate([w_in[:, :, :o_glr], w_in[:, :, o_gout:o_cq]], axis=-1).astype(BF16)
    w_kr = w_in[:, :, o_kr:o_kr + dr]
    w_mla = jnp.concatenate(
        [w_in[:, :, o_cq:o_kr], w_kr, _rotate_half_cols(w_kr, dr), w_in[:, :, o_glr:o_gout],
         jnp.zeros((depth, d, LANES - rank), F32)], axis=-1).astype(BF16)
    wgk = jnp.concatenate([w_gk_up, jnp.zeros((depth, LANES - rank, hk), F32)], axis=1).astype(BF16)
    wq_rope = w_q_up[..., dn:]
    wq = jnp.concatenate([w_q_up, _rotate_half_cols(wq_rope, dr)], axis=-1)
    wq = jnp.transpose(wq, (0, 2, 1, 3)).astype(BF16)
    w_uk2 = w_uk.reshape(depth, kvl, hm * dn)
    w_uv2 = w_uv.reshape(depth, kvl, hm * dvm)
    w_out_b = w_out.astype(BF16)
    wr_t = jnp.transpose(w_router).astype(F32)
    bias_b = jnp.broadcast_to(router_bias.astype(F32)[:, None], (n_exp, LANES))
    pos_rows = jnp.concatenate([jnp.tile(jnp.arange(s_len), batch),
                                jnp.tile(past_len + jnp.arange(t_len), dec_batch)])
    cs = _rope_table(pos_rows, dr)
    consts_p = _gla_constants(GLA_CHUNK, GLA_CHUNK, dv)
    consts_s = _gla_constants(GLA_CHUNK, SAMPLE_PAD, dv)
    new_pad = 16

    x = jnp.concatenate([x_prompt.reshape(n_p, d), x_sample.reshape(n_s, d)], axis=0)
    xb = x.astype(BF16)
    outs = [[] for _ in range(6)]
    for l in range(depth):
        g3 = _mm_slab(xb, w_gla[l])
        cq_n, ckv_n, k_rope, glr = _mla_in(xb, w_mla[l], cs, q_norm_g[l][None], kv_norm_g[l][None],
                                           ql, kvl, dr)
        bgk_l = b_gk[l][None]
        gn_l = gla_norm_g[l][None]
        o_gla_p, st_p = _gla_prompt(g3, glr, wgk[l], bgk_l, gn_l, consts_p, batch, s_len, hg, dk, dv)
        pad_t = ((0, 0), (0, 0), (0, SAMPLE_PAD - t_len), (0, 0))
        g3s = jnp.pad(g3[:, n_p:, :].reshape(-1, dec_batch, t_len, LANES), pad_t)
        g3s = g3s.reshape(-1, dec_batch * SAMPLE_PAD, LANES)
        glr_s = jnp.pad(glr[n_p:].reshape(dec_batch, t_len, LANES), pad_t[1:])
        glr_s = glr_s.reshape(dec_batch * SAMPLE_PAD, LANES)
        o_gla_s, st_s = _gla_sample(g3s, glr_s, wgk[l], bgk_l, gn_l, consts_s, state_gla, l,
                                    dec_batch, t_len, hg, dk, dv)
        o_gla_s = o_gla_s.reshape(hg * nv, dec_batch, SAMPLE_PAD, LANES)[:, :, :t_len]
        o_gla = jnp.concatenate([o_gla_p, o_gla_s.reshape(hg * nv, n_s, LANES)], axis=1)
        q3 = _q_up(cq_n, wq[l], cs, dn, dr, mla_scale)
        k3, v3 = _kv_up(ckv_n, k_rope, w_uk2[l], w_uv2[l], n_p, hm, dn, dvm, dr)
        o_mla_p = _attn_prompt(q3, k3, v3, batch, s_len)
        qlat = _q_lat(q3, w_uk2[l], n_p // n_s, n_s, dn)

        def per_seq(a):
            w = a.shape[-1]
            a = a.reshape(hm, dec_batch, t_len, w)
            return jnp.transpose(a, (1, 0, 2, 3)).reshape(dec_batch, hm * t_len, w)

        pad_n = ((0, 0), (0, new_pad - t_len), (0, 0))
        c_new = jnp.pad(ckv_n[n_p:].reshape(dec_batch, t_len, kvl), pad_n)
        k_new = jnp.pad(k_rope[n_p:].reshape(dec_batch, t_len, dr), pad_n)
        o_lat = _attn_sample(page_table, per_seq(qlat), per_seq(q3[:, n_p:, dn:]), c_new, k_new,
                             cache_kv_latent, cache_k_rope, l, t_len)
        o_lat3 = jnp.transpose(o_lat.reshape(dec_batch, hm, t_len, kvl), (1, 0, 2, 3))
        o_mla_s = _o_uv(o_lat3.reshape(hm, n_s, kvl), w_uv2[l], dvm)
        mixed3 = jnp.concatenate([o_gla, jnp.concatenate([o_mla_p, o_mla_s], axis=1)], axis=0)
        x1, x1b = _out_ln(mixed3, w_out_b[l], x, ln1_g[l][None], ln1_b[l][None], alpha)
        eidx, wts, rank_in, cnt = _router(x1, wr_t, bias_b, N_GROUPS)
        plan = _route_plan(eidx, rank_in, cnt, EXPERT_TILE, n_tok)
        xs = _dispatch(plan[0], plan[5], x1, EXPERT_TILE, plan[6] * EXPERT_TILE)
        hs = _expert_up(plan, xs, w_gate, w_up, l, EXPERT_TILE)
        ys = _expert_down(plan, hs, w_down, l, EXPERT_TILE)
        x, xb = _combine_ln(plan[0], x1, jnp.transpose(wts), ln2_g[l][None], ln2_b[l][None], ys, alpha)
        del x1b
        outs[0].append(st_p)
        outs[1].append(st_s)
        outs[2].append(ckv_n[:n_p].reshape(batch, s_len, kvl))
        outs[3].append(k_rope[:n_p].reshape(batch, s_len, dr))
        outs[4].append(ckv_n[n_p:].reshape(dec_batch, t_len, kvl))
        outs[5].append(k_rope[n_p:].reshape(dec_batch, t_len, dr))
    return (x[:n_p].reshape(batch, s_len, d), x[n_p:].reshape(dec_batch, t_len, d),
            jnp.stack(outs[0]), jnp.stack(outs[1]), jnp.stack(outs[2]), jnp.stack(outs[3]),
            jnp.stack(outs[4]), jnp.stack(outs[5]))
```

```python
import functools
import math

import numpy as np
import jax
import jax.numpy as jnp
from jax import lax
from jax.experimental import pallas as pl
from jax.experimental.pallas import tpu as pltpu

F32 = jnp.float32
BF16 = jnp.bfloat16

GLA_GATE_TAU = 16.0
ROPE_THETA = 10000.0
N_GROUPS = 4
TOP_K = 2
LN_EPS = 1e-5
RMS_EPS = 1e-6

LANES = 128
VMEM_LIMIT = 56 * 1024 * 1024

GLA_CHUNK = 64
SAMPLE_PAD = 8
EXPERT_TILE = 256


def _pick(n, prefs):
    for p in prefs:
        if p <= n and n % p == 0:
            return p
    return n


def _cparams(sem, vmem=VMEM_LIMIT):
    return pltpu.CompilerParams(dimension_semantics=sem, vmem_limit_bytes=vmem)


def _dot(a, b):
    return jnp.dot(a, b, preferred_element_type=F32)


def _dot_nt(a, b):
    return lax.dot_general(a, b, (((1,), (1,)), ((), ())), preferred_element_type=F32)


def _dot_tn(a, b):
    return lax.dot_general(a, b, (((0,), (0,)), ((), ())), preferred_element_type=F32)


def _split2(x):
    hi = x.astype(BF16)
    return hi, (x - hi.astype(F32)).astype(BF16)


def _silu(x):
    return x * (1.0 / (1.0 + jnp.exp(-x)))


def _layer_norm(y, g, b):
    mu = jnp.mean(y, axis=-1, keepdims=True)
    yc = y - mu
    var = jnp.mean(yc * yc, axis=-1, keepdims=True)
    return yc * lax.rsqrt(var + LN_EPS) * g + b


def _rms_norm(y, g):
    return y * lax.rsqrt(jnp.mean(y * y, axis=-1, keepdims=True) + RMS_EPS) * g


def _mm_slab_kernel(a_ref, b_ref, o_ref, acc_ref):
    k = pl.program_id(2)

    @pl.when(k == 0)
    def _():
        acc_ref[...] = jnp.zeros_like(acc_ref)

    acc_ref[...] += _dot(a_ref[...], b_ref[...])

    @pl.when(k == pl.num_programs(2) - 1)
    def _():
        for j in range(o_ref.shape[0]):
            o_ref[j] = acc_ref[:, j * LANES:(j + 1) * LANES].astype(o_ref.dtype)


def _mm_slab(a, b):
    m, kd = a.shape
    n = b.shape[1]
    tm = _pick(m, (2176, 1088, 1024, 512, 256, 128, 64, 32, 16))
    tn = _pick(n, (1536, 1024, 512, 256, 128))
    tk = _pick(kd, (1024, 512, 256, 128))
    return pl.pallas_call(
        _mm_slab_kernel,
        grid=(n // tn, m // tm, kd // tk),
        in_specs=[pl.BlockSpec((tm, tk), lambda j, i, k: (i, k)),
                  pl.BlockSpec((tk, tn), lambda j, i, k: (k, j))],
        out_specs=pl.BlockSpec((tn // LANES, tm, LANES), lambda j, i, k: (j, i, 0)),
        out_shape=jax.ShapeDtypeStruct((n // LANES, m, LANES), BF16),
        scratch_shapes=[pltpu.VMEM((tm, tn), F32)],
        compiler_params=_cparams(("parallel", "parallel", "arbitrary")),
        name="in_proj_gla",
    )(a, b)


def _mla_in_kernel(a_ref, b_ref, cs_ref, qg_ref, kvg_ref,
                   cq_ref, ckv_ref, kr_ref, glr_ref, acc_ref, *, ql, kvl, dr):
    k = pl.program_id(1)

    @pl.when(k == 0)
    def _():
        acc_ref[...] = jnp.zeros_like(acc_ref)

    acc_ref[...] += _dot(a_ref[...], b_ref[...])

    @pl.when(k == pl.num_programs(1) - 1)
    def _():
        cq = acc_ref[:, 0:ql]
        cq_ref[...] = _rms_norm(cq, qg_ref[...]).astype(cq_ref.dtype)
        ckv = acc_ref[:, ql:ql + kvl]
        ckv_ref[...] = _rms_norm(ckv, kvg_ref[...])
        u = acc_ref[:, ql + kvl:ql + kvl + LANES] * cs_ref[...]
        kr_ref[...] = (u + pltpu.roll(u, dr, 1))[:, 0:dr]
        glr_ref[...] = acc_ref[:, ql + kvl + LANES:ql + kvl + 2 * LANES].astype(glr_ref.dtype)


def _mla_in(xb, wm, cs, qg, kvg, ql, kvl, dr):
    m, kd = xb.shape
    n = wm.shape[1]
    tm = _pick(m, (512, 256, 128, 64, 32, 16))
    tk = _pick(kd, (1024, 512, 256, 128))
    kern = functools.partial(_mla_in_kernel, ql=ql, kvl=kvl, dr=dr)
    return pl.pallas_call(
        kern,
        grid=(m // tm, kd // tk),
        in_specs=[pl.BlockSpec((tm, tk), lambda i, k: (i, k)),
                  pl.BlockSpec((tk, n), lambda i, k: (k, 0)),
                  pl.BlockSpec((tm, LANES), lambda i, k: (i, 0)),
                  pl.BlockSpec((1, ql), lambda i, k: (0, 0)),
                  pl.BlockSpec((1, kvl), lambda i, k: (0, 0))],
        out_specs=[pl.BlockSpec((tm, ql), lambda i, k: (i, 0)),
                   pl.BlockSpec((tm, kvl), lambda i, k: (i, 0)),
                   pl.BlockSpec((tm, dr), lambda i, k: (i, 0)),
                   pl.BlockSpec((tm, LANES), lambda i, k: (i, 0))],
        out_shape=[jax.ShapeDtypeStruct((m, ql), BF16),
                   jax.ShapeDtypeStruct((m, kvl), F32),
                   jax.ShapeDtypeStruct((m, dr), F32),
                   jax.ShapeDtypeStruct((m, LANES), BF16)],
        scratch_shapes=[pltpu.VMEM((tm, n), F32)],
        compiler_params=_cparams(("parallel", "arbitrary")),
        name="in_proj_mla",
    )(xb, wm, cs, qg, kvg)


def _q_up_kernel(cq_ref, w_ref, cs_ref, o_ref, *, dn, dr, scale):
    r = _dot(cq_ref[...], w_ref[...])
    o_ref[:, 0:dn] = (r[:, 0:dn] * scale).astype(o_ref.dtype)
    u = r[:, dn:dn + 2 * dr] * cs_ref[...]
    rot = u + pltpu.roll(u, dr, 1)
    o_ref[:, dn:dn + dr] = (rot[:, 0:dr] * scale).astype(o_ref.dtype)


def _q_up(cq, wq, cs, dn, dr, scale):
    m, ql = cq.shape
    h = wq.shape[0]
    tm = _pick(m, (1088, 1024, 512, 256, 128, 64, 32, 16))
    kern = functools.partial(_q_up_kernel, dn=dn, dr=dr, scale=scale)
    return pl.pallas_call(
        kern,
        grid=(m // tm, h),
        in_specs=[pl.BlockSpec((tm, ql), lambda i, j: (i, 0)),
                  pl.BlockSpec((None, ql, dn + 2 * dr), lambda i, j: (j, 0, 0)),
                  pl.BlockSpec((tm, LANES), lambda i, j: (i, 0))],
        out_specs=pl.BlockSpec((None, tm, dn + dr), lambda i, j: (j, i, 0)),
        out_shape=jax.ShapeDtypeStruct((h, m, dn + dr), BF16),
        compiler_params=_cparams(("parallel", "arbitrary")),
        name="mla_q_up",
    )(cq, wq, cs)


def _kv_up_kernel(c_ref, kr_ref, wk_ref, wv_ref, k_ref, v_ref, *, dn, dr):
    c = c_ref[...].astype(BF16)
    k_ref[:, 0:dn] = _dot(c, wk_ref[...].astype(BF16)).astype(k_ref.dtype)
    k_ref[:, dn:dn + dr] = kr_ref[...].astype(k_ref.dtype)
    v_ref[...] = _dot(c, wv_ref[...].astype(BF16)).astype(v_ref.dtype)


def _kv_up(ckv, kr, w_uk2, w_uv2, n_rows, h, dn, dv, dr):
    kvl = ckv.shape[1]
    tm = _pick(n_rows, (1024, 512, 256, 128, 64, 32, 16))
    kern = functools.partial(_kv_up_kernel, dn=dn, dr=dr)
    return pl.pallas_call(
        kern,
        grid=(n_rows // tm, h),
        in_specs=[pl.BlockSpec((tm, kvl), lambda i, j: (i, 0)),
                  pl.BlockSpec((tm, dr), lambda i, j: (i, 0)),
                  pl.BlockSpec((kvl, dn), lambda i, j: (0, j)),
                  pl.BlockSpec((kvl, dv), lambda i, j: (0, j))],
        out_specs=[pl.BlockSpec((None, tm, dn + dr), lambda i, j: (j, i, 0)),
                   pl.BlockSpec((None, tm, dv), lambda i, j: (j, i, 0))],
        out_shape=[jax.ShapeDtypeStruct((h, n_rows, dn + dr), BF16),
                   jax.ShapeDtypeStruct((h, n_rows, dv), BF16)],
        compiler_params=_cparams(("parallel", "arbitrary")),
        name="mla_kv_up",
    )(ckv, kr, w_uk2, w_uv2)


def _attn_prompt_kernel(q_ref, k_ref, v_ref, o_ref, *, tq):
    s_len = q_ref.shape[0]
    for i in range(s_len // tq):
        hi = (i + 1) * tq
        q = q_ref[i * tq:hi, :]
        s = _dot_nt(q, k_ref[0:hi, :])
        row = lax.broadcasted_iota(jnp.int32, s.shape, 0) + i * tq
        col = lax.broadcasted_iota(jnp.int32, s.shape, 1)
        s = jnp.where(col <= row, s, -jnp.inf)
        m = jnp.max(s, axis=-1, keepdims=True)
        p = jnp.exp(s - m)
        l = jnp.sum(p, axis=-1, keepdims=True)
        o = _dot(p.astype(BF16), v_ref[0:hi, :])
        o_ref[i * tq:hi, :] = (o / l).astype(o_ref.dtype)


def _attn_prompt(q3, k3, v3, batch, s_len):
    h, _, dqk = q3.shape
    dv = v3.shape[2]
    tq = _pick(s_len, (256, 128, 64, 32, 16))
    kern = functools.partial(_attn_prompt_kernel, tq=tq)
    return pl.pallas_call(
        kern,
        grid=(batch, h),
        in_specs=[pl.BlockSpec((None, s_len, dqk), lambda b, j: (j, b, 0)),
                  pl.BlockSpec((None, s_len, dqk), lambda b, j: (j, b, 0)),
                  pl.BlockSpec((None, s_len, dv), lambda b, j: (j, b, 0))],
        out_specs=pl.BlockSpec((None, s_len, dv), lambda b, j: (j, b, 0)),
        out_shape=jax.ShapeDtypeStruct((h, batch * s_len, dv), BF16),
        compiler_params=_cparams(("parallel", "parallel")),
        name="mla_attn_prompt",
    )(q3, k3, v3)


def _q_lat_kernel(q_ref, w_ref, o_ref, *, dn):
    o_ref[...] = _dot_nt(q_ref[:, 0:dn], w_ref[...].astype(BF16)).astype(o_ref.dtype)


def _q_lat(q3, w_uk2, row_block, n_rows, dn):
    h, _, dqk = q3.shape
    kvl = w_uk2.shape[0]
    kern = functools.partial(_q_lat_kernel, dn=dn)
    return pl.pallas_call(
        kern,
        grid=(h,),
        in_specs=[pl.BlockSpec((None, n_rows, dqk), lambda j: (j, row_block, 0)),
                  pl.BlockSpec((kvl, dn), lambda j: (0, j))],
        out_specs=pl.BlockSpec((None, n_rows, kvl), lambda j: (j, 0, 0)),
        out_shape=jax.ShapeDtypeStruct((h, n_rows, kvl), BF16),
        compiler_params=_cparams(("parallel",)),
        name="mla_q_absorb",
    )(q3, w_uk2)


def _o_uv_kernel(o_ref, w_ref, y_ref):
    y_ref[...] = _dot(o_ref[...], w_ref[...].astype(BF16)).astype(y_ref.dtype)


def _o_uv(o_lat3, w_uv2, dv):
    h, n_rows, kvl = o_lat3.shape
    return pl.pallas_call(
        _o_uv_kernel,
        grid=(h,),
        in_specs=[pl.BlockSpec((None, n_rows, kvl), lambda j: (j, 0, 0)),
                  pl.BlockSpec((kvl, dv), lambda j: (0, j))],
        out_specs=pl.BlockSpec((None, n_rows, dv), lambda j: (j, 0, 0)),
        out_shape=jax.ShapeDtypeStruct((h, n_rows, dv), BF16),
        compiler_params=_cparams(("parallel",)),
        name="mla_o_absorb",
    )(o_lat3, w_uv2)


def _attn_sample_kernel(pt_ref, ql_ref, qr_ref, cn_ref, kn_ref, cc_hbm, kc_hbm, o_ref,
                        cbuf, kbuf, sem, *, layer, n_pages, page, t_real):
    b = pl.program_id(0)
    nb = pl.num_programs(0)
    slot = b % 2

    def copies(bb, sl, p):
        pg = pt_ref[bb * n_pages + p]
        dst = pl.ds(p * page, page)
        return (pltpu.make_async_copy(cc_hbm.at[layer, pg], cbuf.at[sl, dst], sem.at[0, sl]),
                pltpu.make_async_copy(kc_hbm.at[layer, pg], kbuf.at[sl, :, dst], sem.at[1, sl]))

    def start(bb, sl):
        for p in range(n_pages):
            for c in copies(bb, sl, p):
                c.start()

    @pl.when(b == 0)
    def _():
        start(0, 0)

    @pl.when(b + 1 < nb)
    def _():
        start(b + 1, 1 - slot)

    for p in range(n_pages):
        for c in copies(b, slot, p):
            c.wait()

    cb = cbuf[slot].astype(BF16)
    kb = kbuf[slot].astype(BF16)
    ql = ql_ref[...]
    qr = qr_ref[...]
    cn = cn_ref[...].astype(BF16)
    kn = kn_ref[...].astype(BF16)
    s = _dot_nt(ql, cb) + _dot(qr, kb)
    sn = _dot_nt(ql, cn) + _dot_nt(qr, kn)
    t_of_row = lax.broadcasted_iota(jnp.int32, sn.shape, 0) % t_real
    j = lax.broadcasted_iota(jnp.int32, sn.shape, 1)
    sn = jnp.where(j <= t_of_row, sn, -jnp.inf)
    m = jnp.maximum(jnp.max(s, axis=-1, keepdims=True), jnp.max(sn, axis=-1, keepdims=True))
    p = jnp.exp(s - m)
    pn = jnp.exp(sn - m)
    l = jnp.sum(p, axis=-1, keepdims=True) + jnp.sum(pn, axis=-1, keepdims=True)
    o = _dot(p.astype(BF16), cb) + _dot(pn.astype(BF16), cn)
    o_ref[...] = (o / l).astype(o_ref.dtype)


def _attn_sample(page_table, qlat, qrope, c_new, k_new, cache_c, cache_k, layer, t_real):
    db, ht, kvl = qlat.shape
    dr = qrope.shape[2]
    n_pages = page_table.shape[1]
    page = cache_c.shape[2]
    past = n_pages * page
    tn = c_new.shape[1]
    assert cache_k.shape[2:] == (dr, page)
    kern = functools.partial(_attn_sample_kernel, layer=layer, n_pages=n_pages, page=page,
                             t_real=t_real)
    grid_spec = pltpu.PrefetchScalarGridSpec(
        num_scalar_prefetch=1,
        grid=(db,),
        in_specs=[pl.BlockSpec((None, ht, kvl), lambda b, pt: (b, 0, 0)),
                  pl.BlockSpec((None, ht, dr), lambda b, pt: (b, 0, 0)),
                  pl.BlockSpec((None, tn, kvl), lambda b, pt: (b, 0, 0)),
                  pl.BlockSpec((None, tn, dr), lambda b, pt: (b, 0, 0)),
                  pl.BlockSpec(memory_space=pl.ANY),
                  pl.BlockSpec(memory_space=pl.ANY)],
        out_specs=pl.BlockSpec((None, ht, kvl), lambda b, pt: (b, 0, 0)),
        scratch_shapes=[pltpu.VMEM((2, past, kvl), F32),
                        pltpu.VMEM((2, dr, past), F32),
                        pltpu.SemaphoreType.DMA((2, 2))],
    )
    return pl.pallas_call(
        kern,
        grid_spec=grid_spec,
        out_shape=jax.ShapeDtypeStruct((db, ht, kvl), BF16),
        compiler_params=_cparams(("arbitrary",)),
        name="mla_attn_sample",
    )(page_table.reshape(-1), qlat, qrope, c_new, k_new, cache_c, cache_k)


def _gla_constants(c, g, dv):
    nlev = int(math.log2(g))
    ng = c // g
    r = np.arange(c)
    t = r[:, None]
    j = r[None, :]
    same = (t // g) == (j // g)
    d = [same & (j <= t), same & (j > t)]
    m = [np.eye(c, dtype=bool)]
    for lvl in range(1, nlev + 1):
        size = 1 << lvl
        half = size >> 1
        blk = r // size
        mid = (blk * size + half - 1)[:, None]
        upper = ((r % size) >= half)[:, None]
        d.append(np.where(upper, (j > mid) & (j <= t), (j > t) & (j <= mid)))
        m.append((blk[:, None] == blk[None, :]) & upper & ~upper.T)
    col_group = np.arange(ng * dv)[None, :] // dv
    sel = (t == col_group * g + g - 1)
    bd = (t // g) == col_group
    return (jnp.asarray(np.concatenate(d, 0), BF16), jnp.asarray(np.stack(m), F32),
            jnp.asarray(sel, BF16), jnp.asarray(bd, BF16))


def _gla_chunk(q, k, v, go, glr, wgk, bgk, gn, dmat, m_ref, sel, bd, s_cat, *, c, g, t_real, dk):
    ng = c // g
    nlev = int(math.log2(g))
    dv = v.shape[1]
    x = _dot(glr, wgk) + bgk
    lg = (jnp.minimum(x, 0.0) - jnp.log1p(jnp.exp(-jnp.abs(x)))) * (1.0 / GLA_GATE_TAU)
    if t_real < g:
        rr = lax.broadcasted_iota(jnp.int32, lg.shape, 0) % g
        lg = jnp.where(rr < t_real, lg, 0.0)
    l2 = _split2(lg)
    ex = jnp.exp(_dot(dmat, l2[0]) + _dot(dmat, l2[1]))
    eb = ex[0:c]
    ekd = ex[c:2 * c]
    qf = q.astype(F32) * (dk ** -0.5)
    kf = k.astype(F32)
    attn = m_ref[0] * _dot_nt(qf.astype(BF16), k)
    for lvl in range(1, nlev + 1):
        e = ex[(1 + lvl) * c:(2 + lvl) * c]
        attn = attn + m_ref[lvl] * _dot_nt((qf * e).astype(BF16), (kf * e).astype(BF16))
    o = _dot(attn.astype(BF16), v)
    oi = _dot((qf * eb).astype(BF16), s_cat.astype(BF16))
    if ng == 1:
        o = o + oi
        vd = v
    else:
        o = o + jnp.concatenate(
            [oi[gi * g:(gi + 1) * g, gi * dv:(gi + 1) * dv] for gi in range(ng)], axis=0)
        vd = jnp.concatenate([v] * ng, axis=1) * bd
    e2 = _split2(eb)
    dec = _dot_tn(e2[0], sel) + _dot_tn(e2[1], sel)
    s_new = s_cat * dec + _dot_tn((kf * ekd).astype(BF16), vd)
    out = _rms_norm(o, gn) * _silu(go.astype(F32))
    return out, s_new


def _slabs(ref, rows):
    return jnp.concatenate([ref[j, rows, :] for j in range(ref.shape[0])], axis=1)


def _gla_prompt_kernel(q_ref, k_ref, v_ref, go_ref, glr_ref, wgk_ref, bgk_ref, gn_ref,
                       d_ref, m_ref, sel_ref, bd_ref, o_ref, s_ref, st_ref, *, c, dk):
    @pl.when(pl.program_id(2) == 0)
    def _():
        st_ref[...] = jnp.zeros_like(st_ref)

    state = st_ref[...]
    for i in range(q_ref.shape[0] // c):
        rows = slice(i * c, (i + 1) * c)
        out, state = _gla_chunk(
            q_ref[rows, :], k_ref[rows, :], _slabs(v_ref, rows), _slabs(go_ref, rows),
            glr_ref[rows, :], wgk_ref[...], bgk_ref[...], gn_ref[...], d_ref[...], m_ref,
            sel_ref[...], bd_ref[...], state, c=c, g=c, t_real=c, dk=dk)
        for j in range(o_ref.shape[0]):
            o_ref[j, rows, :] = out[:, j * LANES:(j + 1) * LANES].astype(o_ref.dtype)
    st_ref[...] = state
    s_ref[...] = state


def _gla_prompt(g3, glr, wgk, bgk, gn, consts, batch, s_len, hg, dk, dv):
    c = GLA_CHUNK
    nv = dv // LANES
    tb = _pick(s_len, (512, 256, 128, 64))
    nb = s_len // tb
    dmat, mmat, sel, bd = consts
    kern = functools.partial(_gla_prompt_kernel, c=c, dk=dk)
    rowmap = lambda off: (lambda b, h, t: (off + h, b * nb + t, 0))
    const2 = lambda b, h, t: (0, 0)
    return pl.pallas_call(
        kern,
        grid=(batch, hg, nb),
        in_specs=[pl.BlockSpec((None, tb, LANES), rowmap(0)),
                  pl.BlockSpec((None, tb, LANES), rowmap(hg)),
                  pl.BlockSpec((nv, tb, LANES), rowmap(2 * hg // nv)),
                  pl.BlockSpec((nv, tb, LANES), rowmap(2 * hg // nv + hg)),
                  pl.BlockSpec((tb, LANES), lambda b, h, t: (b * nb + t, 0)),
                  pl.BlockSpec((LANES, dk), lambda b, h, t: (0, h)),
                  pl.BlockSpec((1, dk), lambda b, h, t: (0, h)),
                  pl.BlockSpec((1, dv), lambda b, h, t: (0, h)),
                  pl.BlockSpec(dmat.shape, const2),
                  pl.BlockSpec(mmat.shape, lambda b, h, t: (0, 0, 0)),
                  pl.BlockSpec(sel.shape, const2),
                  pl.BlockSpec(bd.shape, const2)],
        out_specs=[pl.BlockSpec((nv, tb, LANES), lambda b, h, t: (h, b * nb + t, 0)),
                   pl.BlockSpec((None, None, dk, dv), lambda b, h, t: (b, h, 0, 0))],
        out_shape=[jax.ShapeDtypeStruct((hg * nv, batch * s_len, LANES), BF16),
                   jax.ShapeDtypeStruct((batch, hg, dk, dv), F32)],
        scratch_shapes=[pltpu.VMEM((dk, dv), F32)],
        compiler_params=_cparams(("parallel", "parallel", "arbitrary")),
        name="gla_prompt",
    )(g3, g3, g3, g3, glr, wgk, bgk, gn, dmat, mmat, sel, bd)


def _gla_sample_kernel(q_ref, k_ref, v_ref, go_ref, glr_ref, wgk_ref, bgk_ref, gn_ref,
                       d_ref, m_ref, sel_ref, bd_ref, s0_ref, o_ref, s_ref, *, c, g, t_real, dk):
    ng = c // g
    dv = s0_ref.shape[2]
    rows = slice(None)
    s_cat = jnp.concatenate([s0_ref[gi] for gi in range(ng)], axis=1)
    out, s_new = _gla_chunk(
        q_ref[...], k_ref[...], _slabs(v_ref, rows), _slabs(go_ref, rows), glr_ref[...],
        wgk_ref[...], bgk_ref[...], gn_ref[...], d_ref[...], m_ref, sel_ref[...], bd_ref[...],
        s_cat, c=c, g=g, t_real=t_real, dk=dk)
    for j in range(o_ref.shape[0]):
        o_ref[j] = out[:, j * LANES:(j + 1) * LANES].astype(o_ref.dtype)
    for gi in range(ng):
        s_ref[gi] = s_new[:, gi * dv:(gi + 1) * dv]


def _gla_sample(g3s, glr_s, wgk, bgk, gn, consts, state, layer, dec_batch, t_real, hg, dk, dv):
    c = GLA_CHUNK
    g = SAMPLE_PAD
    ng = c // g
    nv = dv // LANES
    dmat, mmat, sel, bd = consts
    kern = functools.partial(_gla_sample_kernel, c=c, g=g, t_real=t_real, dk=dk)
    rowmap = lambda off: (lambda i, h: (off + h, i, 0))
    const2 = lambda i, h: (0, 0)
    return pl.pallas_call(
        kern,
        grid=(dec_batch // ng, hg),
        in_specs=[pl.BlockSpec((None, c, LANES), rowmap(0)),
                  pl.BlockSpec((None, c, LANES), rowmap(hg)),
                  pl.BlockSpec((nv, c, LANES), rowmap(2 * hg // nv)),
                  pl.BlockSpec((nv, c, LANES), rowmap(2 * hg // nv + hg)),
                  pl.BlockSpec((c, LANES), lambda i, h: (i, 0)),
                  pl.BlockSpec((LANES, dk), lambda i, h: (0, h)),
                  pl.BlockSpec((1, dk), lambda i, h: (0, h)),
                  pl.BlockSpec((1, dv), lambda i, h: (0, h)),
                  pl.BlockSpec(dmat.shape, const2),
                  pl.BlockSpec(mmat.shape, lambda i, h: (0, 0, 0)),
                  pl.BlockSpec(sel.shape, const2),
                  pl.BlockSpec(bd.shape, const2),
                  pl.BlockSpec((None, ng, None, dk, dv), lambda i, h: (layer, i, h, 0, 0))],
        out_specs=[pl.BlockSpec((nv, c, LANES), lambda i, h: (h, i, 0)),
                   pl.BlockSpec((ng, None, dk, dv), lambda i, h: (i, h, 0, 0))],
        out_shape=[jax.ShapeDtypeStruct((hg * nv, dec_batch * g, LANES), BF16),
                   jax.ShapeDtypeStruct((dec_batch, hg, dk, dv), F32)],
        compiler_params=_cparams(("parallel", "parallel")),
        name="gla_sample",
    )(g3s, g3s, g3s, g3s, glr_s, wgk, bgk, gn, dmat, mmat, sel, bd, state)


def _out_ln_kernel(gp_ref, gs_ref, mp_ref, ms_ref, b_ref, x_ref, g_ref, bb_ref, of_ref, acc_ref,
                   *, alpha, n_ip, nkg):
    i = pl.program_id(0)
    k = pl.program_id(1)

    @pl.when(k == 0)
    def _():
        acc_ref[...] = jnp.zeros_like(acc_ref)

    def accumulate(a_ref):
        a = jnp.concatenate([a_ref[j] for j in range(a_ref.shape[0])], axis=1)
        acc_ref[...] += _dot(a, b_ref[...])

    for is_s, is_m, ref in ((False, False, gp_ref), (True, False, gs_ref),
                            (False, True, mp_ref), (True, True, ms_ref)):
        @pl.when(((i >= n_ip) == is_s) & ((k >= nkg) == is_m))
        def _():
            accumulate(ref)

    @pl.when(k == pl.num_programs(1) - 1)
    def _():
        of_ref[...] = _layer_norm(alpha * x_ref[...] + acc_ref[...], g_ref[...], bb_ref[...])


def _out_ln(o_gla_p, o_gla_s, o_mla_p, o_mla_s, w_out, x, g, b, alpha):
    sg, n_p, _ = o_gla_p.shape
    sm = o_mla_p.shape[0]
    n_s = o_gla_s.shape[1]
    m, d = x.shape
    tm = _pick(math.gcd(n_p, n_s), (512, 256, 128, 64, 32, 16))
    tk = _pick(math.gcd(sg, sm) * LANES, (256, 128))
    ts = tk // LANES
    n_ip = n_p // tm
    nkg, nkm = sg // ts, sm // ts

    def amap(sample, mla):
        def index(i, k):
            kk = jnp.clip(k - nkg, 0, nkm - 1) if mla else jnp.minimum(k, nkg - 1)
            if sample:
                return jnp.where(i >= n_ip, kk, 0), jnp.maximum(i - n_ip, 0), 0
            return jnp.where(i >= n_ip, 0, kk), jnp.minimum(i, n_ip - 1), 0
        return index

    kern = functools.partial(_out_ln_kernel, alpha=alpha, n_ip=n_ip, nkg=nkg)
    return pl.pallas_call(
        kern,
        grid=(m // tm, nkg + nkm),
        in_specs=[pl.BlockSpec((ts, tm, LANES), amap(False, False)),
                  pl.BlockSpec((ts, tm, LANES), amap(True, False)),
                  pl.BlockSpec((ts, tm, LANES), amap(False, True)),
                  pl.BlockSpec((ts, tm, LANES), amap(True, True)),
                  pl.BlockSpec((tk, d), lambda i, k: (k, 0)),
                  pl.BlockSpec((tm, d), lambda i, k: (i, 0)),
                  pl.BlockSpec((1, d), lambda i, k: (0, 0)),
                  pl.BlockSpec((1, d), lambda i, k: (0, 0))],
        out_specs=pl.BlockSpec((tm, d), lambda i, k: (i, 0)),
        out_shape=jax.ShapeDtypeStruct((m, d), F32),
        scratch_shapes=[pltpu.VMEM((tm, d), F32)],
        compiler_params=_cparams(("parallel", "arbitrary")),
        name="out_proj_ln",
    )(o_gla_p, o_gla_s, o_mla_p, o_mla_s, w_out, x, g, b)


def _first_max(vals):
    best = vals[0]
    idx = jnp.zeros(best.shape, jnp.int32)
    for j in range(1, len(vals)):
        upd = vals[j] > best
        idx = jnp.where(upd, j, idx)
        best = jnp.where(upd, vals[j], best)
    return best, idx


def _pick_row(vals, idx):
    out = vals[0]
    for j in range(1, len(vals)):
        out = jnp.where(idx == j, vals[j], out)
    return out


def _router_kernel(x_ref, wr_ref, bias_ref, tri_ref, eidx_ref, w_ref, rank_ref, cnt_ref,
                   run_ref, *, n_exp, n_groups):
    @pl.when(pl.program_id(0) == 0)
    def _():
        run_ref[...] = jnp.zeros_like(run_ref)

    epg = n_exp // n_groups
    logits = lax.dot_general(wr_ref[...], x_ref[...], (((1,), (1,)), ((), ())),
                             precision=lax.Precision.HIGHEST, preferred_element_type=F32)
    sc = 1.0 / (1.0 + jnp.exp(-logits))
    sel = sc + bias_ref[:, 0:1]
    sel_rows = [sel[e:e + 1, :] for e in range(n_exp)]
    sc_rows = [sc[e:e + 1, :] for e in range(n_exp)]
    group_scores = []
    for gi in range(n_groups):
        r = sel_rows[gi * epg:(gi + 1) * epg]
        best = None
        for a in range(epg):
            for b in range(a + 1, epg):
                s = r[a] + r[b]
                best = s if best is None else jnp.maximum(best, s)
        group_scores.append(best)
    _, g_idx = _first_max(group_scores)
    in_sel = [_pick_row([sel_rows[gi * epg + j] for gi in range(n_groups)], g_idx)
              for j in range(epg)]
    in_sc = [_pick_row([sc_rows[gi * epg + j] for gi in range(n_groups)], g_idx)
             for j in range(epg)]
    _, l1 = _first_max(in_sel)
    _, l2 = _first_max([jnp.where(l1 == j, -jnp.inf, in_sel[j]) for j in range(epg)])
    w1 = _pick_row(in_sc, l1)
    w2 = _pick_row(in_sc, l2)
    den = w1 + w2
    e1 = g_idx * epg + l1
    e2 = g_idx * epg + l2
    eiota = lax.broadcasted_iota(jnp.int32, sc.shape, 0)
    oh1 = (eiota == e1).astype(F32)
    oh2 = (eiota == e2).astype(F32)
    oh = oh1 + oh2
    before = run_ref[:, 0:1] + _dot(oh.astype(BF16), tri_ref[...])
    eidx_ref[0:1, :] = e1
    eidx_ref[1:2, :] = e2
    w_ref[0:1, :] = w1 / den
    w_ref[1:2, :] = w2 / den
    rank_ref[0:1, :] = jnp.sum(oh1 * before, axis=0, keepdims=True).astype(jnp.int32)
    rank_ref[1:2, :] = jnp.sum(oh2 * before, axis=0, keepdims=True).astype(jnp.int32)
    run_ref[...] += jnp.sum(oh, axis=1, keepdims=True)
    cnt_ref[...] = run_ref[...].astype(jnp.int32)


def _router(x, wr_t, bias, n_groups):
    m, d = x.shape
    n_exp = wr_t.shape[0]
    tm = _pick(m, (512, 256, 128))
    tri = jnp.asarray(np.triu(np.ones((tm, tm), np.float32), 1), BF16)
    kern = functools.partial(_router_kernel, n_exp=n_exp, n_groups=n_groups)
    return pl.pallas_call(
        kern,
        grid=(m // tm,),
        in_specs=[pl.BlockSpec((tm, d), lambda i: (i, 0)),
                  pl.BlockSpec((n_exp, d), lambda i: (0, 0)),
                  pl.BlockSpec((n_exp, LANES), lambda i: (0, 0)),
                  pl.BlockSpec((tm, tm), lambda i: (0, 0))],
        out_specs=[pl.BlockSpec((TOP_K, tm), lambda i: (0, i)),
                   pl.BlockSpec((TOP_K, tm), lambda i: (0, i)),
                   pl.BlockSpec((TOP_K, tm), lambda i: (0, i)),
                   pl.BlockSpec((n_exp, LANES), lambda i: (0, 0))],
        out_shape=[jax.ShapeDtypeStruct((TOP_K, m), jnp.int32),
                   jax.ShapeDtypeStruct((TOP_K, m), F32),
                   jax.ShapeDtypeStruct((TOP_K, m), jnp.int32),
                   jax.ShapeDtypeStruct((n_exp, LANES), jnp.int32)],
        scratch_shapes=[pltpu.VMEM((n_exp, LANES), F32)],
        compiler_params=_cparams(("arbitrary",)),
        name="moe_router",
    )(x, wr_t, bias, tri)


def _route_plan(eidx, rank, cnt, tile, n_tok):
    n_exp = cnt.shape[0]
    counts = cnt[:, 0]
    padded = ((counts + tile - 1) // tile) * tile
    ends = jnp.cumsum(padded)
    offs = ends - padded
    e_ids = jnp.arange(n_exp, dtype=jnp.int32)
    is_e = eidx[None] == e_ids[:, None, None]
    pos = (jnp.sum(jnp.where(is_e, offs[:, None, None], 0), axis=0) + rank).astype(jnp.int32)
    t_max = (TOP_K * n_tok) // tile + n_exp
    tiles = jnp.arange(t_max, dtype=jnp.int32)
    n_valid = (ends[-1] // tile).astype(jnp.int32)
    valid = tiles < n_valid
    row_block = jnp.where(valid, tiles, n_valid - 1)
    tile_exp = jnp.minimum(jnp.sum(ends[None, :] <= (row_block * tile)[:, None], axis=1),
                           n_exp - 1).astype(jnp.int32)
    tile_off = jnp.sum(jnp.where(tile_exp[:, None] == e_ids[None, :], offs[None, :], 0), axis=1)
    first = (valid & (tiles * tile == tile_off)).astype(jnp.int32)
    zero_row = jnp.where(padded > 0, ends - tile, -1).astype(jnp.int32)
    return pos.reshape(-1), tile_exp, row_block, first, n_valid.reshape(1), zero_row, t_max


def _dispatch_kernel(pos_ref, zrow_ref, x_ref, xs_hbm, zbuf, sem, zsem, *, tm, n_tok, tile, n_exp):
    i = pl.program_id(0)

    def zero_copy(e):
        r0 = pl.multiple_of(jnp.maximum(zrow_ref[e], 0), tile)
        return pltpu.make_async_copy(zbuf, xs_hbm.at[pl.ds(r0, tile)], zsem)

    @pl.when(i == 0)
    def _():
        zbuf[...] = jnp.zeros_like(zbuf)
        for e in range(n_exp):
            @pl.when(zrow_ref[e] >= 0)
            def _():
                zero_copy(e).start()
        for e in range(n_exp):
            @pl.when(zrow_ref[e] >= 0)
            def _():
                zero_copy(e).wait()

    def row_copy(r, s):
        p = pos_ref[s * n_tok + i * tm + r]
        return pltpu.make_async_copy(x_ref.at[pl.ds(r, 1)], xs_hbm.at[pl.ds(p, 1)], sem)

    def issue(r, carry):
        for s in range(TOP_K):
            row_copy(r, s).start()
        return carry

    def drain(r, carry):
        for s in range(TOP_K):
            row_copy(r, s).wait()
        return carry

    lax.fori_loop(0, tm, issue, 0)
    lax.fori_loop(0, tm, drain, 0)


def _dispatch(pos, zero_row, x, tile, n_rows):
    n_tok, d = x.shape
    n_exp = zero_row.shape[0]
    tm = _pick(n_tok, (256, 128, 64, 32, 16))
    kern = functools.partial(_dispatch_kernel, tm=tm, n_tok=n_tok, tile=tile, n_exp=n_exp)
    grid_spec = pltpu.PrefetchScalarGridSpec(
        num_scalar_prefetch=2,
        grid=(n_tok // tm,),
        in_specs=[pl.BlockSpec((tm, d), lambda i, p, z: (i, 0))],
        out_specs=pl.BlockSpec(memory_space=pl.ANY),
        scratch_shapes=[pltpu.VMEM((tile, d), x.dtype),
                        pltpu.SemaphoreType.DMA(()),
                        pltpu.SemaphoreType.DMA(())],
    )
    return pl.pallas_call(
        kern,
        grid_spec=grid_spec,
        out_shape=jax.ShapeDtypeStruct((n_rows, d), x.dtype),
        compiler_params=_cparams(("arbitrary",)),
        name="moe_dispatch",
    )(pos, zero_row, x)


def _expert_up_kernel(te_ref, rb_ref, first_ref, nv_ref, x_ref, wg_ref, wu_ref, h_ref, wgb, wub):
    t = pl.program_id(1)

    @pl.when(t < nv_ref[0])
    def _():
        @pl.when(first_ref[t] == 1)
        def _():
            wgb[...] = wg_ref[...].astype(BF16)
            wub[...] = wu_ref[...].astype(BF16)

        xb = x_ref[...].astype(BF16)
        h_ref[...] = (_silu(_dot(xb, wgb[...])) * _dot(xb, wub[...])).astype(h_ref.dtype)


def _expert_up(plan, xs, w_gate, w_up, layer, tile):
    _, tile_exp, row_block, first, n_valid, _, t_max = plan
    n_rows, d = xs.shape
    f = w_gate.shape[3]
    tf = _pick(f, (512, 256, 128))
    wmap = lambda j, t, te, rb, fi, nv: (layer, te[t], 0, j)
    grid_spec = pltpu.PrefetchScalarGridSpec(
        num_scalar_prefetch=4,
        grid=(f // tf, t_max),
        in_specs=[pl.BlockSpec((tile, d), lambda j, t, te, rb, fi, nv: (rb[t], 0)),
                  pl.BlockSpec((None, None, d, tf), wmap),
                  pl.BlockSpec((None, None, d, tf), wmap)],
        out_specs=pl.BlockSpec((tile, tf), lambda j, t, te, rb, fi, nv: (rb[t], j)),
        scratch_shapes=[pltpu.VMEM((d, tf), BF16), pltpu.VMEM((d, tf), BF16)],
    )
    return pl.pallas_call(
        _expert_up_kernel,
        grid_spec=grid_spec,
        out_shape=jax.ShapeDtypeStruct((n_rows, f), BF16),
        compiler_params=_cparams(("arbitrary", "arbitrary"), 60 * 1024 * 1024),
        name="moe_expert_up",
    )(tile_exp, row_block, first, n_valid, xs, w_gate, w_up)


def _expert_down_kernel(te_ref, rb_ref, first_ref, nv_ref, h_ref, wd_ref, y_ref, wdb):
    t = pl.program_id(1)

    @pl.when(t < nv_ref[0])
    def _():
        @pl.when(first_ref[t] == 1)
        def _():
            wdb[...] = wd_ref[...].astype(BF16)

        y_ref[...] = _dot(h_ref[...], wdb[...])


def _expert_down(plan, hs, w_down, layer, tile):
    _, tile_exp, row_block, first, n_valid, _, t_max = plan
    n_rows, f = hs.shape
    d = w_down.shape[3]
    tn = _pick(d, (2048, 1024, 512, 256, 128))
    grid_spec = pltpu.PrefetchScalarGridSpec(
        num_scalar_prefetch=4,
        grid=(d // tn, t_max),
        in_specs=[pl.BlockSpec((tile, f), lambda j, t, te, rb, fi, nv: (rb[t], 0)),
                  pl.BlockSpec((None, None, f, tn), lambda j, t, te, rb, fi, nv: (layer, te[t], 0, j))],
        out_specs=pl.BlockSpec((tile, tn), lambda j, t, te, rb, fi, nv: (rb[t], j)),
        scratch_shapes=[pltpu.VMEM((f, tn), BF16)],
    )
    return pl.pallas_call(
        _expert_down_kernel,
        grid_spec=grid_spec,
        out_shape=jax.ShapeDtypeStruct((n_rows, d), F32),
        compiler_params=_cparams(("arbitrary", "arbitrary")),
        name="moe_expert_down",
    )(tile_exp, row_block, first, n_valid, hs, w_down)


def _combine_ln_kernel(pos_ref, x_ref, w_ref, g_ref, b_ref, ys_hbm, of_ref, ob_ref, buf, sem,
                       *, tm, n_tok, alpha):
    i = pl.program_id(0)

    def row_copy(r, s):
        p = pos_ref[s * n_tok + i * tm + r]
        return pltpu.make_async_copy(ys_hbm.at[pl.ds(p, 1)], buf.at[s, pl.ds(r, 1)], sem.at[s])

    def issue(r, carry):
        for s in range(TOP_K):
            row_copy(r, s).start()
        return carry

    def drain(r, carry):
        for s in range(TOP_K):
            row_copy(r, s).wait()
        return carry

    lax.fori_loop(0, tm, issue, 0)
    lax.fori_loop(0, tm, drain, 0)
    y = w_ref[:, 0:1] * buf[0] + w_ref[:, 1:2] * buf[1]
    z = _layer_norm(alpha * x_ref[...] + y, g_ref[...], b_ref[...])
    of_ref[...] = z
    ob_ref[...] = z.astype(ob_ref.dtype)


def _combine_ln(pos, x, wts, g, b, ys, alpha):
    m, d = x.shape
    tm = _pick(m, (256, 128, 64, 32, 16))
    kern = functools.partial(_combine_ln_kernel, tm=tm, n_tok=m, alpha=alpha)
    grid_spec = pltpu.PrefetchScalarGridSpec(
        num_scalar_prefetch=1,
        grid=(m // tm,),
        in_specs=[pl.BlockSpec((tm, d), lambda i, p: (i, 0)),
                  pl.BlockSpec((tm, TOP_K), lambda i, p: (i, 0)),
                  pl.BlockSpec((1, d), lambda i, p: (0, 0)),
                  pl.BlockSpec((1, d), lambda i, p: (0, 0)),
                  pl.BlockSpec(memory_space=pl.ANY)],
        out_specs=[pl.BlockSpec((tm, d), lambda i, p: (i, 0)),
                   pl.BlockSpec((tm, d), lambda i, p: (i, 0))],
        scratch_shapes=[pltpu.VMEM((TOP_K, tm, d), F32),
                        pltpu.SemaphoreType.DMA((TOP_K,))],
    )
    return pl.pallas_call(
        kern,
        grid_spec=grid_spec,
        out_shape=[jax.ShapeDtypeStruct((m, d), F32), jax.ShapeDtypeStruct((m, d), BF16)],
        compiler_params=_cparams(("arbitrary",)),
        name="moe_combine_ln",
    )(pos, x, wts, g, b, ys)


def _rope_table(pos, dr):
    half = dr // 2
    inv_freq = ROPE_THETA ** (-(jnp.arange(half, dtype=F32) * 2.0) / dr)
    ang = pos.astype(F32)[:, None] * inv_freq[None, :]
    cos, sin = jnp.cos(ang), jnp.sin(ang)
    return jnp.concatenate([cos, cos, sin, sin], axis=-1)


def _rotate_half_cols(w, dr):
    half = dr // 2
    return jnp.concatenate([-w[..., half:], w[..., :half]], axis=-1)


def kernel(x_prompt, x_sample, cache_kv_latent, cache_k_rope, state_gla, page_table, w_in, w_gk_up, b_gk, gla_norm_g, q_norm_g, kv_norm_g, w_q_up, w_uk, w_uv, w_out, ln1_g, ln1_b, ln2_g, ln2_b, w_router, router_bias, w_gate, w_up, w_down):
    batch, s_len, d = x_prompt.shape
    dec_batch, t_len, _ = x_sample.shape
    depth = w_in.shape[0]
    _, _, hg, dk, dv = state_gla.shape
    rank = w_gk_up.shape[1]
    ql = q_norm_g.shape[1]
    kvl = kv_norm_g.shape[1]
    hm, dn = w_uk.shape[2], w_uk.shape[3]
    dvm = w_uv.shape[3]
    dr = w_q_up.shape[3] - dn
    n_exp = w_router.shape[1]
    page = cache_kv_latent.shape[2]
    past_len = page_table.shape[1] * page
    n_p = batch * s_len
    n_s = dec_batch * t_len
    n_tok = n_p + n_s
    assert dk == LANES and dv % LANES == 0 and 2 * dr == LANES and dn == LANES and dvm == LANES
    assert rank <= LANES and n_p % n_s == 0 and t_len <= SAMPLE_PAD
    alpha = (2.0 * depth) ** 0.25
    mla_scale = (dn + dr) ** -0.5
    hk, hv = hg * dk, hg * dv
    nv = dv // LANES

    o_glr = 2 * hk + hv
    o_gout = o_glr + rank
    o_cq = o_gout + hv
    o_kr = o_cq + ql + kvl
    w_gla = jnp.concatenate([w_in[:, :, :o_glr], w_in[:, :, o_gout:o_cq]], axis=-1).astype(BF16)
    w_kr = w_in[:, :, o_kr:o_kr + dr]
    w_mla = jnp.concatenate(
        [w_in[:, :, o_cq:o_kr], w_kr, _rotate_half_cols(w_kr, dr), w_in[:, :, o_glr:o_gout],
         jnp.zeros((depth, d, LANES - rank), F32)], axis=-1).astype(BF16)
    wgk = jnp.concatenate([w_gk_up, jnp.zeros((depth, LANES - rank, hk), F32)], axis=1).astype(BF16)
    wq_rope = w_q_up[..., dn:]
    wq = jnp.concatenate([w_q_up, _rotate_half_cols(wq_rope, dr)], axis=-1)
    wq = jnp.transpose(wq, (0, 2, 1, 3)).astype(BF16)
    w_uk2 = w_uk.reshape(depth, kvl, hm * dn)
    w_uv2 = w_uv.reshape(depth, kvl, hm * dvm)
    w_out_b = w_out.astype(BF16)
    wr_t = jnp.transpose(w_router).astype(F32)
    bias_b = jnp.broadcast_to(router_bias.astype(F32)[:, None], (n_exp, LANES))
    pos_rows = jnp.concatenate([jnp.tile(jnp.arange(s_len), batch),
                                jnp.tile(past_len + jnp.arange(t_len), dec_batch)])
    cs = _rope_table(pos_rows, dr)
    consts_p = _gla_constants(GLA_CHUNK, GLA_CHUNK, dv)
    consts_s = _gla_constants(GLA_CHUNK, SAMPLE_PAD, dv)
    new_pad = 16
    cache_k_t = jnp.swapaxes(cache_k_rope, 2, 3)

    x = jnp.concatenate([x_prompt.reshape(n_p, d), x_sample.reshape(n_s, d)], axis=0)
    xb = x.astype(BF16)
    outs = [[] for _ in range(6)]
    for l in range(depth):
        g3 = _mm_slab(xb, w_gla[l])
        cq_n, ckv_n, k_rope, glr = _mla_in(xb, w_mla[l], cs, q_norm_g[l][None], kv_norm_g[l][None],
                                           ql, kvl, dr)
        bgk_l = b_gk[l][None]
        gn_l = gla_norm_g[l][None]
        o_gla_p, st_p = _gla_prompt(g3, glr, wgk[l], bgk_l, gn_l, consts_p, batch, s_len, hg, dk, dv)
        pad_t = ((0, 0), (0, 0), (0, SAMPLE_PAD - t_len), (0, 0))
        g3s = jnp.pad(g3[:, n_p:, :].reshape(-1, dec_batch, t_len, LANES), pad_t)
        g3s = g3s.reshape(-1, dec_batch * SAMPLE_PAD, LANES)
        glr_s = jnp.pad(glr[n_p:].reshape(dec_batch, t_len, LANES), pad_t[1:])
        glr_s = glr_s.reshape(dec_batch * SAMPLE_PAD, LANES)
        o_gla_s, st_s = _gla_sample(g3s, glr_s, wgk[l], bgk_l, gn_l, consts_s, state_gla, l,
                                    dec_batch, t_len, hg, dk, dv)
        o_gla_s = o_gla_s.reshape(hg * nv, dec_batch, SAMPLE_PAD, LANES)[:, :, :t_len]
        o_gla_s = o_gla_s.reshape(hg * nv, n_s, LANES)
        q3 = _q_up(cq_n, wq[l], cs, dn, dr, mla_scale)
        k3, v3 = _kv_up(ckv_n, k_rope, w_uk2[l], w_uv2[l], n_p, hm, dn, dvm, dr)
        o_mla_p = _attn_prompt(q3, k3, v3, batch, s_len)
        qlat = _q_lat(q3, w_uk2[l], n_p // n_s, n_s, dn)

        def per_seq(a):
            w = a.shape[-1]
            a = a.reshape(hm, dec_batch, t_len, w)
            return jnp.transpose(a, (1, 0, 2, 3)).reshape(dec_batch, hm * t_len, w)

        pad_n = ((0, 0), (0, new_pad - t_len), (0, 0))
        c_new = jnp.pad(ckv_n[n_p:].reshape(dec_batch, t_len, kvl), pad_n)
        k_new = jnp.pad(k_rope[n_p:].reshape(dec_batch, t_len, dr), pad_n)
        o_lat = _attn_sample(page_table, per_seq(qlat), per_seq(q3[:, n_p:, dn:]), c_new, k_new,
                             cache_kv_latent, cache_k_t, l, t_len)
        o_lat3 = jnp.transpose(o_lat.reshape(dec_batch, hm, t_len, kvl), (1, 0, 2, 3))
        o_mla_s = _o_uv(o_lat3.reshape(hm, n_s, kvl), w_uv2[l], dvm)
        x1 = _out_ln(o_gla_p, o_gla_s, o_mla_p, o_mla_s, w_out_b[l], x, ln1_g[l][None],
                     ln1_b[l][None], alpha)
        eidx, wts, rank_in, cnt = _router(x1, wr_t, bias_b, N_GROUPS)
        plan = _route_plan(eidx, rank_in, cnt, EXPERT_TILE, n_tok)
        xs = _dispatch(plan[0], plan[5], x1, EXPERT_TILE, plan[6] * EXPERT_TILE)
        hs = _expert_up(plan, xs, w_gate, w_up, l, EXPERT_TILE)
        ys = _expert_down(plan, hs, w_down, l, EXPERT_TILE)
        x, xb = _combine_ln(plan[0], x1, jnp.transpose(wts), ln2_g[l][None], ln2_b[l][None], ys, alpha)
        outs[0].append(st_p)
        outs[1].append(st_s)
        outs[2].append(ckv_n[:n_p].reshape(batch, s_len, kvl))
        outs[3].append(k_rope[:n_p].reshape(batch, s_len, dr))
        outs[4].append(ckv_n[n_p:].reshape(dec_batch, t_len, kvl))
        outs[5].append(k_rope[n_p:].reshape(dec_batch, t_len, dr))
    return (x[:n_p].reshape(batch, s_len, d), x[n_p:].reshape(dec_batch, t_len, d),
            jnp.stack(outs[0]), jnp.stack(outs[1]), jnp.stack(outs[2]), jnp.stack(outs[3]),
            jnp.stack(outs[4]), jnp.stack(outs[5]))
```

```python
import functools
import math

import numpy as np
import jax
import jax.numpy as jnp
from jax import lax
from jax.experimental import pallas as pl
from jax.experimental.pallas import tpu as pltpu

F32 = jnp.float32
BF16 = jnp.bfloat16

GLA_GATE_TAU = 16.0
ROPE_THETA = 10000.0
N_GROUPS = 4
TOP_K = 2
LN_EPS = 1e-5
RMS_EPS = 1e-6

LANES = 128
VMEM_LIMIT = 56 * 1024 * 1024

GLA_CHUNK = 64
GLA_PROMPT_CHUNK = 128
SAMPLE_PAD = 8
EXPERT_TILE = 256


def _pick(n, prefs):
    for p in prefs:
        if p <= n and n % p == 0:
            return p
    return n


def _cparams(sem, vmem=VMEM_LIMIT):
    return pltpu.CompilerParams(dimension_semantics=sem, vmem_limit_bytes=vmem)


def _dot(a, b):
    return jnp.dot(a, b, preferred_element_type=F32)


def _dot_nt(a, b):
    return lax.dot_general(a, b, (((1,), (1,)), ((), ())), preferred_element_type=F32)


def _dot_tn(a, b):
    return lax.dot_general(a, b, (((0,), (0,)), ((), ())), preferred_element_type=F32)


def _split2(x):
    hi = x.astype(BF16)
    return hi, (x - hi.astype(F32)).astype(BF16)


def _silu(x):
    return x * (1.0 / (1.0 + jnp.exp(-x)))


def _layer_norm(y, g, b):
    mu = jnp.mean(y, axis=-1, keepdims=True)
    yc = y - mu
    var = jnp.mean(yc * yc, axis=-1, keepdims=True)
    return yc * lax.rsqrt(var + LN_EPS) * g + b


def _rms_norm(y, g):
    return y * lax.rsqrt(jnp.mean(y * y, axis=-1, keepdims=True) + RMS_EPS) * g


def _mm_slab_kernel(a_ref, b_ref, o_ref, acc_ref):
    k = pl.program_id(2)

    @pl.when(k == 0)
    def _():
        acc_ref[...] = jnp.zeros_like(acc_ref)

    acc_ref[...] += _dot_nt(a_ref[...], b_ref[...])

    @pl.when(k == pl.num_programs(2) - 1)
    def _():
        for j in range(o_ref.shape[0]):
            o_ref[j] = acc_ref[:, j * LANES:(j + 1) * LANES].astype(o_ref.dtype)


def _mm_slab(a, b):
    m, kd = a.shape
    n = b.shape[0]
    tm = _pick(m, (2176, 1088, 1024, 512, 256, 128, 64, 32, 16))
    tn = _pick(n, (1536, 1024, 512, 256, 128))
    tk = _pick(kd, (1024, 512, 256, 128))
    return pl.pallas_call(
        _mm_slab_kernel,
        grid=(n // tn, m // tm, kd // tk),
        in_specs=[pl.BlockSpec((tm, tk), lambda j, i, k: (i, k)),
                  pl.BlockSpec((tn, tk), lambda j, i, k: (j, k))],
        out_specs=pl.BlockSpec((tn // LANES, tm, LANES), lambda j, i, k: (j, i, 0)),
        out_shape=jax.ShapeDtypeStruct((n // LANES, m, LANES), BF16),
        scratch_shapes=[pltpu.VMEM((tm, tn), F32)],
        compiler_params=_cparams(("parallel", "parallel", "arbitrary")),
        name="in_proj_gla",
    )(a, b)


def _mla_in_kernel(a_ref, b_ref, cs_ref, qg_ref, kvg_ref,
                   cq_ref, ckv_ref, kr_ref, glr_ref, acc_ref, *, ql, kvl, dr):
    k = pl.program_id(1)

    @pl.when(k == 0)
    def _():
        acc_ref[...] = jnp.zeros_like(acc_ref)

    acc_ref[...] += _dot_nt(a_ref[...], b_ref[...])

    @pl.when(k == pl.num_programs(1) - 1)
    def _():
        cq = acc_ref[:, 0:ql]
        cq_ref[...] = _rms_norm(cq, qg_ref[...]).astype(cq_ref.dtype)
        ckv = acc_ref[:, ql:ql + kvl]
        ckv_ref[...] = _rms_norm(ckv, kvg_ref[...])
        u = acc_ref[:, ql + kvl:ql + kvl + LANES] * cs_ref[...]
        kr_ref[...] = (u + pltpu.roll(u, dr, 1))[:, 0:dr]
        glr_ref[...] = acc_ref[:, ql + kvl + LANES:ql + kvl + 2 * LANES].astype(glr_ref.dtype)


def _mla_in(xb, wm, cs, qg, kvg, ql, kvl, dr):
    m, kd = xb.shape
    n = wm.shape[0]
    tm = _pick(m, (512, 256, 128, 64, 32, 16))
    tk = _pick(kd, (1024, 512, 256, 128))
    kern = functools.partial(_mla_in_kernel, ql=ql, kvl=kvl, dr=dr)
    return pl.pallas_call(
        kern,
        grid=(m // tm, kd // tk),
        in_specs=[pl.BlockSpec((tm, tk), lambda i, k: (i, k)),
                  pl.BlockSpec((n, tk), lambda i, k: (0, k)),
                  pl.BlockSpec((tm, LANES), lambda i, k: (i, 0)),
                  pl.BlockSpec((1, ql), lambda i, k: (0, 0)),
                  pl.BlockSpec((1, kvl), lambda i, k: (0, 0))],
        out_specs=[pl.BlockSpec((tm, ql), lambda i, k: (i, 0)),
                   pl.BlockSpec((tm, kvl), lambda i, k: (i, 0)),
                   pl.BlockSpec((tm, dr), lambda i, k: (i, 0)),
                   pl.BlockSpec((tm, LANES), lambda i, k: (i, 0))],
        out_shape=[jax.ShapeDtypeStruct((m, ql), BF16),
                   jax.ShapeDtypeStruct((m, kvl), F32),
                   jax.ShapeDtypeStruct((m, dr), F32),
                   jax.ShapeDtypeStruct((m, LANES), BF16)],
        scratch_shapes=[pltpu.VMEM((tm, n), F32)],
        compiler_params=_cparams(("parallel", "arbitrary")),
        name="in_proj_mla",
    )(xb, wm, cs, qg, kvg)


def _q_up_kernel(cq_ref, w_ref, cs_ref, o_ref, *, dn, dr, scale):
    r = _dot(cq_ref[...], w_ref[...])
    o_ref[:, 0:dn] = (r[:, 0:dn] * scale).astype(o_ref.dtype)
    u = r[:, dn:dn + 2 * dr] * cs_ref[...]
    rot = u + pltpu.roll(u, dr, 1)
    o_ref[:, dn:dn + dr] = (rot[:, 0:dr] * scale).astype(o_ref.dtype)


def _q_up(cq, wq, cs, dn, dr, scale):
    m, ql = cq.shape
    h = wq.shape[0]
    tm = _pick(m, (1088, 1024, 512, 256, 128, 64, 32, 16))
    kern = functools.partial(_q_up_kernel, dn=dn, dr=dr, scale=scale)
    return pl.pallas_call(
        kern,
        grid=(m // tm, h),
        in_specs=[pl.BlockSpec((tm, ql), lambda i, j: (i, 0)),
                  pl.BlockSpec((None, ql, dn + 2 * dr), lambda i, j: (j, 0, 0)),
                  pl.BlockSpec((tm, LANES), lambda i, j: (i, 0))],
        out_specs=pl.BlockSpec((None, tm, dn + dr), lambda i, j: (j, i, 0)),
        out_shape=jax.ShapeDtypeStruct((h, m, dn + dr), BF16),
        compiler_params=_cparams(("parallel", "arbitrary")),
        name="mla_q_up",
    )(cq, wq, cs)


def _kv_up_kernel(c_ref, kr_ref, wk_ref, wv_ref, k_ref, v_ref, *, dn, dr):
    c = c_ref[...].astype(BF16)
    k_ref[:, 0:dn] = _dot(c, wk_ref[...].astype(BF16)).astype(k_ref.dtype)
    k_ref[:, dn:dn + dr] = kr_ref[...].astype(k_ref.dtype)
    v_ref[...] = _dot(c, wv_ref[...].astype(BF16)).astype(v_ref.dtype)


def _kv_up(ckv, kr, w_uk2, w_uv2, n_rows, h, dn, dv, dr):
    kvl = ckv.shape[1]
    tm = _pick(n_rows, (1024, 512, 256, 128, 64, 32, 16))
    kern = functools.partial(_kv_up_kernel, dn=dn, dr=dr)
    return pl.pallas_call(
        kern,
        grid=(n_rows // tm, h),
        in_specs=[pl.BlockSpec((tm, kvl), lambda i, j: (i, 0)),
                  pl.BlockSpec((tm, dr), lambda i, j: (i, 0)),
                  pl.BlockSpec((kvl, dn), lambda i, j: (0, j)),
                  pl.BlockSpec((kvl, dv), lambda i, j: (0, j))],
        out_specs=[pl.BlockSpec((None, tm, dn + dr), lambda i, j: (j, i, 0)),
                   pl.BlockSpec((None, tm, dv), lambda i, j: (j, i, 0))],
        out_shape=[jax.ShapeDtypeStruct((h, n_rows, dn + dr), BF16),
                   jax.ShapeDtypeStruct((h, n_rows, dv), BF16)],
        compiler_params=_cparams(("parallel", "arbitrary")),
        name="mla_kv_up",
    )(ckv, kr, w_uk2, w_uv2)


def _attn_prompt_kernel(q_ref, k_ref, v_ref, o_ref, *, tq):
    s_len = q_ref.shape[0]
    for i in range(s_len // tq):
        hi = (i + 1) * tq
        q = q_ref[i * tq:hi, :]
        s = _dot_nt(q, k_ref[0:hi, :])
        row = lax.broadcasted_iota(jnp.int32, s.shape, 0) + i * tq
        col = lax.broadcasted_iota(jnp.int32, s.shape, 1)
        s = jnp.where(col <= row, s, -jnp.inf)
        m = jnp.max(s, axis=-1, keepdims=True)
        p = jnp.exp(s - m)
        l = jnp.sum(p, axis=-1, keepdims=True)
        o = _dot(p.astype(BF16), v_ref[0:hi, :])
        o_ref[i * tq:hi, :] = (o / l).astype(o_ref.dtype)


def _attn_prompt(q3, k3, v3, batch, s_len):
    h, _, dqk = q3.shape
    dv = v3.shape[2]
    tq = _pick(s_len, (256, 128, 64, 32, 16))
    kern = functools.partial(_attn_prompt_kernel, tq=tq)
    return pl.pallas_call(
        kern,
        grid=(batch, h),
        in_specs=[pl.BlockSpec((None, s_len, dqk), lambda b, j: (j, b, 0)),
                  pl.BlockSpec((None, s_len, dqk), lambda b, j: (j, b, 0)),
                  pl.BlockSpec((None, s_len, dv), lambda b, j: (j, b, 0))],
        out_specs=pl.BlockSpec((None, s_len, dv), lambda b, j: (j, b, 0)),
        out_shape=jax.ShapeDtypeStruct((h, batch * s_len, dv), BF16),
        compiler_params=_cparams(("parallel", "parallel")),
        name="mla_attn_prompt",
    )(q3, k3, v3)


def _q_lat_kernel(q_ref, w_ref, o_ref, *, dn):
    o_ref[...] = _dot_nt(q_ref[:, 0:dn], w_ref[...].astype(BF16)).astype(o_ref.dtype)


def _q_lat(q3, w_uk2, row_block, n_rows, dn):
    h, _, dqk = q3.shape
    kvl = w_uk2.shape[0]
    kern = functools.partial(_q_lat_kernel, dn=dn)
    return pl.pallas_call(
        kern,
        grid=(h,),
        in_specs=[pl.BlockSpec((None, n_rows, dqk), lambda j: (j, row_block, 0)),
                  pl.BlockSpec((kvl, dn), lambda j: (0, j))],
        out_specs=pl.BlockSpec((None, n_rows, kvl), lambda j: (j, 0, 0)),
        out_shape=jax.ShapeDtypeStruct((h, n_rows, kvl), BF16),
        compiler_params=_cparams(("parallel",)),
        name="mla_q_absorb",
    )(q3, w_uk2)


def _o_uv_kernel(o_ref, w_ref, y_ref):
    y_ref[...] = _dot(o_ref[...], w_ref[...].astype(BF16)).astype(y_ref.dtype)


def _o_uv(o_lat3, w_uv2, dv):
    h, n_rows, kvl = o_lat3.shape
    return pl.pallas_call(
        _o_uv_kernel,
        grid=(h,),
        in_specs=[pl.BlockSpec((None, n_rows, kvl), lambda j: (j, 0, 0)),
                  pl.BlockSpec((kvl, dv), lambda j: (0, j))],
        out_specs=pl.BlockSpec((None, n_rows, dv), lambda j: (j, 0, 0)),
        out_shape=jax.ShapeDtypeStruct((h, n_rows, dv), BF16),
        compiler_params=_cparams(("parallel",)),
        name="mla_o_absorb",
    )(o_lat3, w_uv2)


def _attn_sample_kernel(pt_ref, ql_ref, qr_ref, cn_ref, kn_ref, cc_hbm, kc_hbm, o_ref,
                        cbuf, kbuf, sem, *, layer, n_pages, page, t_real):
    b = pl.program_id(0)
    nb = pl.num_programs(0)
    slot = b % 2

    def copies(bb, sl, p):
        pg = pt_ref[bb * n_pages + p]
        dst = pl.ds(p * page, page)
        return (pltpu.make_async_copy(cc_hbm.at[layer, pg], cbuf.at[sl, dst], sem.at[0, sl]),
                pltpu.make_async_copy(kc_hbm.at[layer, pg], kbuf.at[sl, :, dst], sem.at[1, sl]))

    def start(bb, sl):
        for p in range(n_pages):
            for c in copies(bb, sl, p):
                c.start()

    @pl.when(b == 0)
    def _():
        start(0, 0)

    @pl.when(b + 1 < nb)
    def _():
        start(b + 1, 1 - slot)

    for p in range(n_pages):
        for c in copies(b, slot, p):
            c.wait()

    cb = cbuf[slot].astype(BF16)
    kb = kbuf[slot].astype(BF16)
    ql = ql_ref[...]
    qr = qr_ref[...]
    cn = cn_ref[...].astype(BF16)
    kn = kn_ref[...].astype(BF16)
    s = _dot_nt(ql, cb) + _dot(qr, kb)
    sn = _dot_nt(ql, cn) + _dot_nt(qr, kn)
    t_of_row = lax.broadcasted_iota(jnp.int32, sn.shape, 0) % t_real
    j = lax.broadcasted_iota(jnp.int32, sn.shape, 1)
    sn = jnp.where(j <= t_of_row, sn, -jnp.inf)
    m = jnp.maximum(jnp.max(s, axis=-1, keepdims=True), jnp.max(sn, axis=-1, keepdims=True))
    p = jnp.exp(s - m)
    pn = jnp.exp(sn - m)
    l = jnp.sum(p, axis=-1, keepdims=True) + jnp.sum(pn, axis=-1, keepdims=True)
    o = _dot(p.astype(BF16), cb) + _dot(pn.astype(BF16), cn)
    o_ref[...] = (o / l).astype(o_ref.dtype)


def _attn_sample(page_table, qlat, qrope, c_new, k_new, cache_c, cache_k, layer, t_real):
    db, ht, kvl = qlat.shape
    dr = qrope.shape[2]
    n_pages = page_table.shape[1]
    page = cache_c.shape[2]
    past = n_pages * page
    tn = c_new.shape[1]
    assert cache_k.shape[2:] == (dr, page)
    kern = functools.partial(_attn_sample_kernel, layer=layer, n_pages=n_pages, page=page,
                             t_real=t_real)
    grid_spec = pltpu.PrefetchScalarGridSpec(
        num_scalar_prefetch=1,
        grid=(db,),
        in_specs=[pl.BlockSpec((None, ht, kvl), lambda b, pt: (b, 0, 0)),
                  pl.BlockSpec((None, ht, dr), lambda b, pt: (b, 0, 0)),
                  pl.BlockSpec((None, tn, kvl), lambda b, pt: (b, 0, 0)),
                  pl.BlockSpec((None, tn, dr), lambda b, pt: (b, 0, 0)),
                  pl.BlockSpec(memory_space=pl.ANY),
                  pl.BlockSpec(memory_space=pl.ANY)],
        out_specs=pl.BlockSpec((None, ht, kvl), lambda b, pt: (b, 0, 0)),
        scratch_shapes=[pltpu.VMEM((2, past, kvl), F32),
                        pltpu.VMEM((2, dr, past), F32),
                        pltpu.SemaphoreType.DMA((2, 2))],
    )
    return pl.pallas_call(
        kern,
        grid_spec=grid_spec,
        out_shape=jax.ShapeDtypeStruct((db, ht, kvl), BF16),
        compiler_params=_cparams(("arbitrary",)),
        name="mla_attn_sample",
    )(page_table.reshape(-1), qlat, qrope, c_new, k_new, cache_c, cache_k)


def _gla_constants(c, g, dv):
    nlev = int(math.log2(g))
    ng = c // g
    r = np.arange(c)
    t = r[:, None]
    j = r[None, :]
    same = (t // g) == (j // g)
    d = [same & (j <= t), same & (j > t)]
    m = [np.eye(c, dtype=bool)]
    for lvl in range(1, nlev + 1):
        size = 1 << lvl
        half = size >> 1
        blk = r // size
        mid = (blk * size + half - 1)[:, None]
        upper = ((r % size) >= half)[:, None]
        d.append(np.where(upper, (j > mid) & (j <= t), (j > t) & (j <= mid)))
        m.append((blk[:, None] == blk[None, :]) & upper & ~upper.T)
    col_group = np.arange(ng * dv)[None, :] // dv
    sel = (t == col_group * g + g - 1)
    bd = (t // g) == col_group
    return (jnp.asarray(np.concatenate(d, 0), BF16), jnp.asarray(np.stack(m), F32),
            jnp.asarray(sel, BF16), jnp.asarray(bd, BF16))


def _gla_chunk(q, k, v, go, glr, wgk, bgk, gn, dmat, m_ref, sel, bd, s_cat, *, c, g, t_real, dk):
    ng = c // g
    nlev = int(math.log2(g))
    dv = v.shape[1]
    x = _dot(glr, wgk) + bgk
    lg = (jnp.minimum(x, 0.0) - jnp.log1p(jnp.exp(-jnp.abs(x)))) * (1.0 / GLA_GATE_TAU)
    if t_real < g:
        rr = lax.broadcasted_iota(jnp.int32, lg.shape, 0) % g
        lg = jnp.where(rr < t_real, lg, 0.0)
    l2 = _split2(lg)
    ex = jnp.exp(_dot(dmat, l2[0]) + _dot(dmat, l2[1]))
    eb = ex[0:c]
    ekd = ex[c:2 * c]
    qf = q.astype(F32) * (dk ** -0.5)
    kf = k.astype(F32)
    attn = m_ref[0] * _dot_nt(qf.astype(BF16), k)
    for lvl in range(1, nlev + 1):
        e = ex[(1 + lvl) * c:(2 + lvl) * c]
        attn = attn + m_ref[lvl] * _dot_nt((qf * e).astype(BF16), (kf * e).astype(BF16))
    o = _dot(attn.astype(BF16), v)
    oi = _dot((qf * eb).astype(BF16), s_cat.astype(BF16))
    if ng == 1:
        o = o + oi
        vd = v
    else:
        o = o + jnp.concatenate(
            [oi[gi * g:(gi + 1) * g, gi * dv:(gi + 1) * dv] for gi in range(ng)], axis=0)
        vd = jnp.concatenate([v] * ng, axis=1) * bd
    e2 = _split2(eb)
    dec = _dot_tn(e2[0], sel) + _dot_tn(e2[1], sel)
    s_new = s_cat * dec + _dot_tn((kf * ekd).astype(BF16), vd)
    out = _rms_norm(o, gn) * _silu(go.astype(F32))
    return out, s_new


def _slabs(ref, rows):
    return jnp.concatenate([ref[j, rows, :] for j in range(ref.shape[0])], axis=1)


def _gla_prompt_kernel(q_ref, k_ref, v_ref, go_ref, glr_ref, wgk_ref, bgk_ref, gn_ref,
                       d_ref, m_ref, sel_ref, bd_ref, o_ref, s_ref, st_ref, *, c, dk):
    @pl.when(pl.program_id(2) == 0)
    def _():
        st_ref[...] = jnp.zeros_like(st_ref)

    state = st_ref[...]
    for i in range(q_ref.shape[0] // c):
        rows = slice(i * c, (i + 1) * c)
        out, state = _gla_chunk(
            q_ref[rows, :], k_ref[rows, :], _slabs(v_ref, rows), _slabs(go_ref, rows),
            glr_ref[rows, :], wgk_ref[...], bgk_ref[...], gn_ref[...], d_ref[...], m_ref,
            sel_ref[...], bd_ref[...], state, c=c, g=c, t_real=c, dk=dk)
        for j in range(o_ref.shape[0]):
            o_ref[j, rows, :] = out[:, j * LANES:(j + 1) * LANES].astype(o_ref.dtype)
    st_ref[...] = state
    s_ref[...] = state


def _gla_prompt(g3, glr, wgk, bgk, gn, consts, batch, s_len, hg, dk, dv):
    c = GLA_PROMPT_CHUNK
    nv = dv // LANES
    tb = _pick(s_len, (512, 256, 128))
    nb = s_len // tb
    dmat, mmat, sel, bd = consts
    kern = functools.partial(_gla_prompt_kernel, c=c, dk=dk)
    rowmap = lambda off: (lambda b, h, t: (off + h, b * nb + t, 0))
    const2 = lambda b, h, t: (0, 0)
    return pl.pallas_call(
        kern,
        grid=(batch, hg, nb),
        in_specs=[pl.BlockSpec((None, tb, LANES), rowmap(0)),
                  pl.BlockSpec((None, tb, LANES), rowmap(hg)),
                  pl.BlockSpec((nv, tb, LANES), rowmap(2 * hg // nv)),
                  pl.BlockSpec((nv, tb, LANES), rowmap(2 * hg // nv + hg)),
                  pl.BlockSpec((tb, LANES), lambda b, h, t: (b * nb + t, 0)),
                  pl.BlockSpec((LANES, dk), lambda b, h, t: (0, h)),
                  pl.BlockSpec((1, dk), lambda b, h, t: (0, h)),
                  pl.BlockSpec((1, dv), lambda b, h, t: (0, h)),
                  pl.BlockSpec(dmat.shape, const2),
                  pl.BlockSpec(mmat.shape, lambda b, h, t: (0, 0, 0)),
                  pl.BlockSpec(sel.shape, const2),
                  pl.BlockSpec(bd.shape, const2)],
        out_specs=[pl.BlockSpec((nv, tb, LANES), lambda b, h, t: (h, b * nb + t, 0)),
                   pl.BlockSpec((None, None, dk, dv), lambda b, h, t: (b, h, 0, 0))],
        out_shape=[jax.ShapeDtypeStruct((hg * nv, batch * s_len, LANES), BF16),
                   jax.ShapeDtypeStruct((batch, hg, dk, dv), F32)],
        scratch_shapes=[pltpu.VMEM((dk, dv), F32)],
        compiler_params=_cparams(("parallel", "parallel", "arbitrary")),
        name="gla_prompt",
    )(g3, g3, g3, g3, glr, wgk, bgk, gn, dmat, mmat, sel, bd)


def _gla_sample_kernel(q_ref, k_ref, v_ref, go_ref, glr_ref, wgk_ref, bgk_ref, gn_ref,
                       d_ref, m_ref, sel_ref, bd_ref, s0_ref, *rest, c, g, t_real, dk):
    o_ref, s_ref = rest[-2:]
    ng = c // g
    dv = s0_ref.shape[2]
    rows = slice(None)
    s_cat = jnp.concatenate([s0_ref[gi] for gi in range(ng)], axis=1)
    out, s_new = _gla_chunk(
        q_ref[...], k_ref[...], _slabs(v_ref, rows), _slabs(go_ref, rows), glr_ref[...],
        wgk_ref[...], bgk_ref[...], gn_ref[...], d_ref[...], m_ref, sel_ref[...], bd_ref[...],
        s_cat, c=c, g=g, t_real=t_real, dk=dk)
    for j in range(o_ref.shape[0]):
        o_ref[j] = out[:, j * LANES:(j + 1) * LANES].astype(o_ref.dtype)
    for gi in range(ng):
        s_ref[gi] = s_new[:, gi * dv:(gi + 1) * dv]


def _gla_sample(g3s, glr_s, wgk, bgk, gn, consts, state, new_states, layer, dec_batch, t_real,
                hg, dk, dv):
    c = GLA_CHUNK
    g = SAMPLE_PAD
    ng = c // g
    nv = dv // LANES
    dmat, mmat, sel, bd = consts
    kern = functools.partial(_gla_sample_kernel, c=c, g=g, t_real=t_real, dk=dk)
    rowmap = lambda off: (lambda i, h: (off + h, i, 0))
    const2 = lambda i, h: (0, 0)
    state_spec = pl.BlockSpec((None, ng, None, dk, dv), lambda i, h: (layer, i, h, 0, 0))
    extra_specs, extra_args, aliases = [], [], {}
    if new_states is not None:
        extra_specs, extra_args, aliases = [pl.BlockSpec(memory_space=pl.ANY)], [new_states], {13: 1}
    return pl.pallas_call(
        kern,
        grid=(dec_batch // ng, hg),
        in_specs=[pl.BlockSpec((None, c, LANES), rowmap(0)),
                  pl.BlockSpec((None, c, LANES), rowmap(hg)),
                  pl.BlockSpec((nv, c, LANES), rowmap(2 * hg // nv)),
                  pl.BlockSpec((nv, c, LANES), rowmap(2 * hg // nv + hg)),
                  pl.BlockSpec((c, LANES), lambda i, h: (i, 0)),
                  pl.BlockSpec((LANES, dk), lambda i, h: (0, h)),
                  pl.BlockSpec((1, dk), lambda i, h: (0, h)),
                  pl.BlockSpec((1, dv), lambda i, h: (0, h)),
                  pl.BlockSpec(dmat.shape, const2),
                  pl.BlockSpec(mmat.shape, lambda i, h: (0, 0, 0)),
                  pl.BlockSpec(sel.shape, const2),
                  pl.BlockSpec(bd.shape, const2),
                  state_spec] + extra_specs,
        out_specs=[pl.BlockSpec((nv, c, LANES), lambda i, h: (h, i, 0)), state_spec],
        out_shape=[jax.ShapeDtypeStruct((hg * nv, dec_batch * g, LANES), BF16),
                   jax.ShapeDtypeStruct(state.shape, F32)],
        input_output_aliases=aliases,
        compiler_params=_cparams(("parallel", "parallel")),
        name="gla_sample",
    )(g3s, g3s, g3s, g3s, glr_s, wgk, bgk, gn, dmat, mmat, sel, bd, state, *extra_args)


def _out_ln_kernel(gp_ref, gs_ref, mp_ref, ms_ref, b_ref, x_ref, g_ref, bb_ref, of_ref, acc_ref,
                   *, alpha, n_ip, nkg):
    i = pl.program_id(0)
    k = pl.program_id(1)

    @pl.when(k == 0)
    def _():
        acc_ref[...] = jnp.zeros_like(acc_ref)

    def accumulate(a_ref):
        a = jnp.concatenate([a_ref[j] for j in range(a_ref.shape[0])], axis=1)
        acc_ref[...] += _dot(a, b_ref[...])

    for is_s, is_m, ref in ((False, False, gp_ref), (True, False, gs_ref),
                            (False, True, mp_ref), (True, True, ms_ref)):
        @pl.when(((i >= n_ip) == is_s) & ((k >= nkg) == is_m))
        def _():
            accumulate(ref)

    @pl.when(k == pl.num_programs(1) - 1)
    def _():
        of_ref[...] = _layer_norm(alpha * x_ref[...] + acc_ref[...], g_ref[...], bb_ref[...])


def _out_ln(o_gla_p, o_gla_s, o_mla_p, o_mla_s, w_out, x, g, b, alpha):
    sg, n_p, _ = o_gla_p.shape
    sm = o_mla_p.shape[0]
    n_s = o_gla_s.shape[1]
    m, d = x.shape
    tm = _pick(math.gcd(n_p, n_s), (512, 256, 128, 64, 32, 16))
    tk = _pick(math.gcd(sg, sm) * LANES, (512, 256, 128))
    ts = tk // LANES
    n_ip = n_p // tm
    nkg, nkm = sg // ts, sm // ts

    def amap(sample, mla):
        def index(i, k):
            kk = jnp.clip(k - nkg, 0, nkm - 1) if mla else jnp.minimum(k, nkg - 1)
            if sample:
                return jnp.where(i >= n_ip, kk, 0), jnp.maximum(i - n_ip, 0), 0
            return jnp.where(i >= n_ip, 0, kk), jnp.minimum(i, n_ip - 1), 0
        return index

    kern = functools.partial(_out_ln_kernel, alpha=alpha, n_ip=n_ip, nkg=nkg)
    return pl.pallas_call(
        kern,
        grid=(m // tm, nkg + nkm),
        in_specs=[pl.BlockSpec((ts, tm, LANES), amap(False, False)),
                  pl.BlockSpec((ts, tm, LANES), amap(True, False)),
                  pl.BlockSpec((ts, tm, LANES), amap(False, True)),
                  pl.BlockSpec((ts, tm, LANES), amap(True, True)),
                  pl.BlockSpec((tk, d), lambda i, k: (k, 0)),
                  pl.BlockSpec((tm, d), lambda i, k: (i, 0), pipeline_mode=pl.Buffered(1)),
                  pl.BlockSpec((1, d), lambda i, k: (0, 0)),
                  pl.BlockSpec((1, d), lambda i, k: (0, 0))],
        out_specs=pl.BlockSpec((tm, d), lambda i, k: (i, 0)),
        out_shape=jax.ShapeDtypeStruct((m, d), F32),
        scratch_shapes=[pltpu.VMEM((tm, d), F32)],
        compiler_params=_cparams(("parallel", "arbitrary")),
        name="out_proj_ln",
    )(o_gla_p, o_gla_s, o_mla_p, o_mla_s, w_out, x, g, b)


def _first_max(vals):
    best = vals[0]
    idx = jnp.zeros(best.shape, jnp.int32)
    for j in range(1, len(vals)):
        upd = vals[j] > best
        idx = jnp.where(upd, j, idx)
        best = jnp.where(upd, vals[j], best)
    return best, idx


def _pick_row(vals, idx):
    out = vals[0]
    for j in range(1, len(vals)):
        out = jnp.where(idx == j, vals[j], out)
    return out


def _router_kernel(x_ref, wr_ref, bias_ref, tri_ref, eidx_ref, w_ref, rank_ref, cnt_ref,
                   run_ref, *, n_exp, n_groups):
    @pl.when(pl.program_id(0) == 0)
    def _():
        run_ref[...] = jnp.zeros_like(run_ref)

    epg = n_exp // n_groups
    logits = lax.dot_general(wr_ref[...], x_ref[...], (((1,), (1,)), ((), ())),
                             precision=lax.Precision.HIGHEST, preferred_element_type=F32)
    sc = 1.0 / (1.0 + jnp.exp(-logits))
    sel = sc + bias_ref[:, 0:1]
    sel_rows = [sel[e:e + 1, :] for e in range(n_exp)]
    sc_rows = [sc[e:e + 1, :] for e in range(n_exp)]
    group_scores = []
    for gi in range(n_groups):
        r = sel_rows[gi * epg:(gi + 1) * epg]
        best = None
        for a in range(epg):
            for b in range(a + 1, epg):
                s = r[a] + r[b]
                best = s if best is None else jnp.maximum(best, s)
        group_scores.append(best)
    _, g_idx = _first_max(group_scores)
    in_sel = [_pick_row([sel_rows[gi * epg + j] for gi in range(n_groups)], g_idx)
              for j in range(epg)]
    in_sc = [_pick_row([sc_rows[gi * epg + j] for gi in range(n_groups)], g_idx)
             for j in range(epg)]
    _, l1 = _first_max(in_sel)
    _, l2 = _first_max([jnp.where(l1 == j, -jnp.inf, in_sel[j]) for j in range(epg)])
    w1 = _pick_row(in_sc, l1)
    w2 = _pick_row(in_sc, l2)
    den = w1 + w2
    e1 = g_idx * epg + l1
    e2 = g_idx * epg + l2
    eiota = lax.broadcasted_iota(jnp.int32, sc.shape, 0)
    oh1 = (eiota == e1).astype(F32)
    oh2 = (eiota == e2).astype(F32)
    oh = oh1 + oh2
    before = run_ref[:, 0:1] + _dot(oh.astype(BF16), tri_ref[...])
    eidx_ref[0:1, :] = e1
    eidx_ref[1:2, :] = e2
    w_ref[0:1, :] = w1 / den
    w_ref[1:2, :] = w2 / den
    rank_ref[0:1, :] = jnp.sum(oh1 * before, axis=0, keepdims=True).astype(jnp.int32)
    rank_ref[1:2, :] = jnp.sum(oh2 * before, axis=0, keepdims=True).astype(jnp.int32)
    run_ref[...] += jnp.sum(oh, axis=1, keepdims=True)
    cnt_ref[...] = run_ref[...].astype(jnp.int32)


def _router(x, wr_t, bias, n_groups):
    m, d = x.shape
    n_exp = wr_t.shape[0]
    tm = _pick(m, (512, 256, 128))
    tri = jnp.asarray(np.triu(np.ones((tm, tm), np.float32), 1), BF16)
    kern = functools.partial(_router_kernel, n_exp=n_exp, n_groups=n_groups)
    return pl.pallas_call(
        kern,
        grid=(m // tm,),
        in_specs=[pl.BlockSpec((tm, d), lambda i: (i, 0)),
                  pl.BlockSpec((n_exp, d), lambda i: (0, 0)),
                  pl.BlockSpec((n_exp, LANES), lambda i: (0, 0)),
                  pl.BlockSpec((tm, tm), lambda i: (0, 0))],
        out_specs=[pl.BlockSpec((TOP_K, tm), lambda i: (0, i)),
                   pl.BlockSpec((TOP_K, tm), lambda i: (0, i)),
                   pl.BlockSpec((TOP_K, tm), lambda i: (0, i)),
                   pl.BlockSpec((n_exp, LANES), lambda i: (0, 0))],
        out_shape=[jax.ShapeDtypeStruct((TOP_K, m), jnp.int32),
                   jax.ShapeDtypeStruct((TOP_K, m), F32),
                   jax.ShapeDtypeStruct((TOP_K, m), jnp.int32),
                   jax.ShapeDtypeStruct((n_exp, LANES), jnp.int32)],
        scratch_shapes=[pltpu.VMEM((n_exp, LANES), F32)],
        compiler_params=_cparams(("arbitrary",)),
        name="moe_router",
    )(x, wr_t, bias, tri)


def _route_plan(eidx, rank, cnt, tile, n_tok):
    n_exp = cnt.shape[0]
    counts = cnt[:, 0]
    padded = ((counts + tile - 1) // tile) * tile
    ends = jnp.cumsum(padded)
    offs = ends - padded
    e_ids = jnp.arange(n_exp, dtype=jnp.int32)
    is_e = eidx[None] == e_ids[:, None, None]
    pos = (jnp.sum(jnp.where(is_e, offs[:, None, None], 0), axis=0) + rank).astype(jnp.int32)
    t_max = (TOP_K * n_tok) // tile + n_exp
    tiles = jnp.arange(t_max, dtype=jnp.int32)
    n_valid = (ends[-1] // tile).astype(jnp.int32)
    valid = tiles < n_valid
    row_block = jnp.where(valid, tiles, n_valid - 1)
    tile_exp = jnp.minimum(jnp.sum(ends[None, :] <= (row_block * tile)[:, None], axis=1),
                           n_exp - 1).astype(jnp.int32)
    tile_off = jnp.sum(jnp.where(tile_exp[:, None] == e_ids[None, :], offs[None, :], 0), axis=1)
    first = (valid & (tiles * tile == tile_off)).astype(jnp.int32)
    zero_row = jnp.where(padded > 0, ends - tile, -1).astype(jnp.int32)
    return pos.reshape(-1), tile_exp, row_block, first, n_valid.reshape(1), zero_row, t_max


def _pack_bf16_pairs(x):
    half = x.shape[1] // 2
    hi = lax.bitcast_convert_type(x[:, :half].astype(BF16).astype(F32), jnp.uint32)
    lo = lax.bitcast_convert_type(x[:, half:].astype(BF16).astype(F32), jnp.uint32)
    return hi | (lo >> 16)


def _unpack_bf16_pairs(p):
    hi = lax.bitcast_convert_type(p & jnp.uint32(0xFFFF0000), F32).astype(BF16)
    lo = lax.bitcast_convert_type(p << 16, F32).astype(BF16)
    return hi, lo


def _dispatch_kernel(pos_ref, zrow_ref, x_ref, xs_hbm, pbuf, zbuf, sem, zsem, *, tm, n_tok, tile,
                     n_exp):
    i = pl.program_id(0)
    pbuf[...] = _pack_bf16_pairs(x_ref[...])

    def zero_copy(e):
        r0 = pl.multiple_of(jnp.maximum(zrow_ref[e], 0), tile)
        return pltpu.make_async_copy(zbuf, xs_hbm.at[pl.ds(r0, tile)], zsem)

    @pl.when(i == 0)
    def _():
        zbuf[...] = jnp.zeros_like(zbuf)
        for e in range(n_exp):
            @pl.when(zrow_ref[e] >= 0)
            def _():
                zero_copy(e).start()
        for e in range(n_exp):
            @pl.when(zrow_ref[e] >= 0)
            def _():
                zero_copy(e).wait()

    def row_copy(r, s):
        p = pos_ref[s * n_tok + i * tm + r]
        return pltpu.make_async_copy(pbuf.at[pl.ds(r, 1)], xs_hbm.at[pl.ds(p, 1)], sem)

    def issue(r, carry):
        for s in range(TOP_K):
            row_copy(r, s).start()
        return carry

    def drain(r, carry):
        for s in range(TOP_K):
            row_copy(r, s).wait()
        return carry

    lax.fori_loop(0, tm, issue, 0)
    lax.fori_loop(0, tm, drain, 0)


def _dispatch(pos, zero_row, x, tile, n_rows):
    n_tok, d = x.shape
    n_exp = zero_row.shape[0]
    tm = _pick(n_tok, (256, 128, 64, 32, 16))
    kern = functools.partial(_dispatch_kernel, tm=tm, n_tok=n_tok, tile=tile, n_exp=n_exp)
    grid_spec = pltpu.PrefetchScalarGridSpec(
        num_scalar_prefetch=2,
        grid=(n_tok // tm,),
        in_specs=[pl.BlockSpec((tm, d), lambda i, p, z: (i, 0))],
        out_specs=pl.BlockSpec(memory_space=pl.ANY),
        scratch_shapes=[pltpu.VMEM((tm, d // 2), jnp.uint32),
                        pltpu.VMEM((tile, d // 2), jnp.uint32),
                        pltpu.SemaphoreType.DMA(()),
                        pltpu.SemaphoreType.DMA(())],
    )
    return pl.pallas_call(
        kern,
        grid_spec=grid_spec,
        out_shape=jax.ShapeDtypeStruct((n_rows, d // 2), jnp.uint32),
        compiler_params=_cparams(("arbitrary",)),
        name="moe_dispatch",
    )(pos, zero_row, x)


def _expert_up_kernel(te_ref, rb_ref, first_ref, nv_ref, x_ref, wg_ref, wu_ref, h_ref, wgb, wub):
    t = pl.program_id(1)

    @pl.when(t < nv_ref[0])
    def _():
        @pl.when(first_ref[t] == 1)
        def _():
            wgb[...] = wg_ref[...].astype(BF16)
            wub[...] = wu_ref[...].astype(BF16)

        x_hi, x_lo = _unpack_bf16_pairs(x_ref[...])
        half = x_ref.shape[1]
        g = _dot(x_hi, wgb[0:half, :]) + _dot(x_lo, wgb[half:, :])
        u = _dot(x_hi, wub[0:half, :]) + _dot(x_lo, wub[half:, :])
        h_ref[...] = (_silu(g) * u).astype(h_ref.dtype)


def _expert_up(plan, xs, w_gate, w_up, layer, tile):
    _, tile_exp, row_block, first, n_valid, _, t_max = plan
    n_rows = xs.shape[0]
    d = 2 * xs.shape[1]
    f = w_gate.shape[3]
    tf = _pick(f, (512, 256, 128))
    wmap = lambda j, t, te, rb, fi, nv: (layer, te[t], 0, j)
    grid_spec = pltpu.PrefetchScalarGridSpec(
        num_scalar_prefetch=4,
        grid=(f // tf, t_max),
        in_specs=[pl.BlockSpec((tile, d // 2), lambda j, t, te, rb, fi, nv: (rb[t], 0)),
                  pl.BlockSpec((None, None, d, tf), wmap),
                  pl.BlockSpec((None, None, d, tf), wmap)],
        out_specs=pl.BlockSpec((tile, tf), lambda j, t, te, rb, fi, nv: (rb[t], j)),
        scratch_shapes=[pltpu.VMEM((d, tf), BF16), pltpu.VMEM((d, tf), BF16)],
    )
    return pl.pallas_call(
        _expert_up_kernel,
        grid_spec=grid_spec,
        out_shape=jax.ShapeDtypeStruct((n_rows, f), BF16),
        compiler_params=_cparams(("arbitrary", "arbitrary"), 60 * 1024 * 1024),
        name="moe_expert_up",
    )(tile_exp, row_block, first, n_valid, xs, w_gate, w_up)


def _expert_down_kernel(te_ref, rb_ref, first_ref, nv_ref, h_ref, wd_ref, y_ref, wdb):
    t = pl.program_id(1)

    @pl.when(t < nv_ref[0])
    def _():
        @pl.when(first_ref[t] == 1)
        def _():
            wdb[...] = wd_ref[...].astype(BF16)

        y_ref[...] = _dot(h_ref[...], wdb[...])


def _expert_down(plan, hs, w_down, layer, tile):
    _, tile_exp, row_block, first, n_valid, _, t_max = plan
    n_rows, f = hs.shape
    d = w_down.shape[3]
    tn = _pick(d, (4096, 2048, 1024, 512, 256, 128))
    grid_spec = pltpu.PrefetchScalarGridSpec(
        num_scalar_prefetch=4,
        grid=(d // tn, t_max),
        in_specs=[pl.BlockSpec((tile, f), lambda j, t, te, rb, fi, nv: (rb[t], 0)),
                  pl.BlockSpec((None, None, f, tn), lambda j, t, te, rb, fi, nv: (layer, te[t], 0, j))],
        out_specs=pl.BlockSpec((tile, tn), lambda j, t, te, rb, fi, nv: (rb[t], j)),
        scratch_shapes=[pltpu.VMEM((f, tn), BF16)],
    )
    return pl.pallas_call(
        _expert_down_kernel,
        grid_spec=grid_spec,
        out_shape=jax.ShapeDtypeStruct((n_rows, d), F32),
        compiler_params=_cparams(("arbitrary", "arbitrary")),
        name="moe_expert_down",
    )(tile_exp, row_block, first, n_valid, hs, w_down)


def _combine_ln_kernel(pos_ref, x_ref, w_ref, g_ref, b_ref, ys_hbm, of_ref, ob_ref, buf, sem,
                       *, tm, n_tok, alpha):
    i = pl.program_id(0)
    slot = i % 2

    def row_copy(ii, sl, r, s):
        p = pos_ref[s * n_tok + ii * tm + r]
        return pltpu.make_async_copy(ys_hbm.at[pl.ds(p, 1)], buf.at[sl, s, pl.ds(r, 1)],
                                     sem.at[sl, s])

    def issue(ii, sl):
        def body(r, carry):
            for s in range(TOP_K):
                row_copy(ii, sl, r, s).start()
            return carry
        lax.fori_loop(0, tm, body, 0)

    def drain(ii, sl):
        def body(r, carry):
            for s in range(TOP_K):
                row_copy(ii, sl, r, s).wait()
            return carry
        lax.fori_loop(0, tm, body, 0)

    @pl.when(i == 0)
    def _():
        issue(0, 0)

    @pl.when(i + 1 < pl.num_programs(0))
    def _():
        issue(i + 1, 1 - slot)

    drain(i, slot)
    y = w_ref[:, 0:1] * buf[slot, 0] + w_ref[:, 1:2] * buf[slot, 1]
    z = _layer_norm(alpha * x_ref[...] + y, g_ref[...], b_ref[...])
    of_ref[...] = z
    ob_ref[...] = z.astype(ob_ref.dtype)


def _combine_ln(pos, x, wts, g, b, ys, alpha):
    m, d = x.shape
    tm = _pick(m, (256, 128, 64, 32, 16))
    kern = functools.partial(_combine_ln_kernel, tm=tm, n_tok=m, alpha=alpha)
    grid_spec = pltpu.PrefetchScalarGridSpec(
        num_scalar_prefetch=1,
        grid=(m // tm,),
        in_specs=[pl.BlockSpec((tm, d), lambda i, p: (i, 0)),
                  pl.BlockSpec((tm, TOP_K), lambda i, p: (i, 0)),
                  pl.BlockSpec((1, d), lambda i, p: (0, 0)),
                  pl.BlockSpec((1, d), lambda i, p: (0, 0)),
                  pl.BlockSpec(memory_space=pl.ANY)],
        out_specs=[pl.BlockSpec((tm, d), lambda i, p: (i, 0)),
                   pl.BlockSpec((tm, d), lambda i, p: (i, 0))],
        scratch_shapes=[pltpu.VMEM((2, TOP_K, tm, d), F32),
                        pltpu.SemaphoreType.DMA((2, TOP_K))],
    )
    return pl.pallas_call(
        kern,
        grid_spec=grid_spec,
        out_shape=[jax.ShapeDtypeStruct((m, d), F32), jax.ShapeDtypeStruct((m, d), BF16)],
        compiler_params=_cparams(("arbitrary",)),
        name="moe_combine_ln",
    )(pos, x, wts, g, b, ys)


def _rope_table(pos, dr):
    half = dr // 2
    inv_freq = ROPE_THETA ** (-(jnp.arange(half, dtype=F32) * 2.0) / dr)
    ang = pos.astype(F32)[:, None] * inv_freq[None, :]
    cos, sin = jnp.cos(ang), jnp.sin(ang)
    return jnp.concatenate([cos, cos, sin, sin], axis=-1)


def _rotate_half_cols(w, dr):
    half = dr // 2
    return jnp.concatenate([-w[..., half:], w[..., :half]], axis=-1)


def kernel(x_prompt, x_sample, cache_kv_latent, cache_k_rope, state_gla, page_table, w_in, w_gk_up, b_gk, gla_norm_g, q_norm_g, kv_norm_g, w_q_up, w_uk, w_uv, w_out, ln1_g, ln1_b, ln2_g, ln2_b, w_router, router_bias, w_gate, w_up, w_down):
    batch, s_len, d = x_prompt.shape
    dec_batch, t_len, _ = x_sample.shape
    depth = w_in.shape[0]
    _, _, hg, dk, dv = state_gla.shape
    rank = w_gk_up.shape[1]
    ql = q_norm_g.shape[1]
    kvl = kv_norm_g.shape[1]
    hm, dn = w_uk.shape[2], w_uk.shape[3]
    dvm = w_uv.shape[3]
    dr = w_q_up.shape[3] - dn
    n_exp = w_router.shape[1]
    page = cache_kv_latent.shape[2]
    past_len = page_table.shape[1] * page
    n_p = batch * s_len
    n_s = dec_batch * t_len
    n_tok = n_p + n_s
    assert dk == LANES and dv % LANES == 0 and 2 * dr == LANES and dn == LANES and dvm == LANES
    assert rank <= LANES and n_p % n_s == 0 and t_len <= SAMPLE_PAD
    alpha = (2.0 * depth) ** 0.25
    mla_scale = (dn + dr) ** -0.5
    hk, hv = hg * dk, hg * dv
    nv = dv // LANES

    o_glr = 2 * hk + hv
    o_gout = o_glr + rank
    o_cq = o_gout + hv
    o_kr = o_cq + ql + kvl
    w_in_t = jnp.swapaxes(w_in, 1, 2)
    w_gla = jnp.concatenate([w_in_t[:, :o_glr], w_in_t[:, o_gout:o_cq]], axis=1).astype(BF16)
    w_kr = w_in_t[:, o_kr:o_kr + dr]
    w_kr_rot = jnp.concatenate([-w_kr[:, dr // 2:], w_kr[:, :dr // 2]], axis=1)
    w_mla = jnp.concatenate(
        [w_in_t[:, o_cq:o_kr], w_kr, w_kr_rot, w_in_t[:, o_glr:o_gout],
         jnp.zeros((depth, LANES - rank, d), F32)], axis=1).astype(BF16)
    wgk = jnp.concatenate([w_gk_up, jnp.zeros((depth, LANES - rank, hk), F32)], axis=1).astype(BF16)
    wq_rope = w_q_up[..., dn:]
    wq = jnp.concatenate([w_q_up, _rotate_half_cols(wq_rope, dr)], axis=-1)
    wq = jnp.transpose(wq, (0, 2, 1, 3)).astype(BF16)
    w_uk2 = w_uk.reshape(depth, kvl, hm * dn)
    w_uv2 = w_uv.reshape(depth, kvl, hm * dvm)
    w_out_b = w_out.astype(BF16)
    wr_t = jnp.transpose(w_router).astype(F32)
    bias_b = jnp.broadcast_to(router_bias.astype(F32)[:, None], (n_exp, LANES))
    pos_rows = jnp.concatenate([jnp.tile(jnp.arange(s_len), batch),
                                jnp.tile(past_len + jnp.arange(t_len), dec_batch)])
    cs = _rope_table(pos_rows, dr)
    consts_p = _gla_constants(GLA_PROMPT_CHUNK, GLA_PROMPT_CHUNK, dv)
    consts_s = _gla_constants(GLA_CHUNK, SAMPLE_PAD, dv)
    new_pad = 16
    cache_k_t = jnp.swapaxes(cache_k_rope, 2, 3)

    x = jnp.concatenate([x_prompt.reshape(n_p, d), x_sample.reshape(n_s, d)], axis=0)
    xb = x.astype(BF16)
    outs = [[] for _ in range(6)]
    st_s_all = None
    for l in range(depth):
        g3 = _mm_slab(xb, w_gla[l])
        cq_n, ckv_n, k_rope, glr = _mla_in(xb, w_mla[l], cs, q_norm_g[l][None], kv_norm_g[l][None],
                                           ql, kvl, dr)
        bgk_l = b_gk[l][None]
        gn_l = gla_norm_g[l][None]
        o_gla_p, st_p = _gla_prompt(g3, glr, wgk[l], bgk_l, gn_l, consts_p, batch, s_len, hg, dk, dv)
        pad_t = ((0, 0), (0, 0), (0, SAMPLE_PAD - t_len), (0, 0))
        g3s = jnp.pad(g3[:, n_p:, :].reshape(-1, dec_batch, t_len, LANES), pad_t)
        g3s = g3s.reshape(-1, dec_batch * SAMPLE_PAD, LANES)
        glr_s = jnp.pad(glr[n_p:].reshape(dec_batch, t_len, LANES), pad_t[1:])
        glr_s = glr_s.reshape(dec_batch * SAMPLE_PAD, LANES)
        o_gla_s, st_s_all = _gla_sample(g3s, glr_s, wgk[l], bgk_l, gn_l, consts_s, state_gla,
                                        st_s_all, l, dec_batch, t_len, hg, dk, dv)
        o_gla_s = o_gla_s.reshape(hg * nv, dec_batch, SAMPLE_PAD, LANES)[:, :, :t_len]
        o_gla_s = o_gla_s.reshape(hg * nv, n_s, LANES)
        q3 = _q_up(cq_n, wq[l], cs, dn, dr, mla_scale)
        k3, v3 = _kv_up(ckv_n, k_rope, w_uk2[l], w_uv2[l], n_p, hm, dn, dvm, dr)
        o_mla_p = _attn_prompt(q3, k3, v3, batch, s_len)
        qlat = _q_lat(q3, w_uk2[l], n_p // n_s, n_s, dn)

        def per_seq(a):
            w = a.shape[-1]
            a = a.reshape(hm, dec_batch, t_len, w)
            return jnp.transpose(a, (1, 0, 2, 3)).reshape(dec_batch, hm * t_len, w)

        pad_n = ((0, 0), (0, new_pad - t_len), (0, 0))
        c_new = jnp.pad(ckv_n[n_p:].reshape(dec_batch, t_len, kvl), pad_n)
        k_new = jnp.pad(k_rope[n_p:].reshape(dec_batch, t_len, dr), pad_n)
        o_lat = _attn_sample(page_table, per_seq(qlat), per_seq(q3[:, n_p:, dn:]), c_new, k_new,
                             cache_kv_latent, cache_k_t, l, t_len)
        o_lat3 = jnp.transpose(o_lat.reshape(dec_batch, hm, t_len, kvl), (1, 0, 2, 3))
        o_mla_s = _o_uv(o_lat3.reshape(hm, n_s, kvl), w_uv2[l], dvm)
        x1 = _out_ln(o_gla_p, o_gla_s, o_mla_p, o_mla_s, w_out_b[l], x, ln1_g[l][None],
                     ln1_b[l][None], alpha)
        eidx, wts, rank_in, cnt = _router(x1, wr_t, bias_b, N_GROUPS)
        plan = _route_plan(eidx, rank_in, cnt, EXPERT_TILE, n_tok)
        xs = _dispatch(plan[0], plan[5], x1, EXPERT_TILE, plan[6] * EXPERT_TILE)
        hs = _expert_up(plan, xs, w_gate, w_up, l, EXPERT_TILE)
        ys = _expert_down(plan, hs, w_down, l, EXPERT_TILE)
        x, xb = _combine_ln(plan[0], x1, jnp.transpose(wts), ln2_g[l][None], ln2_b[l][None], ys, alpha)
        outs[0].append(st_p)
        outs[2].append(ckv_n[:n_p].reshape(batch, s_len, kvl))
        outs[3].append(k_rope[:n_p].reshape(batch, s_len, dr))
        outs[4].append(ckv_n[n_p:].reshape(dec_batch, t_len, kvl))
        outs[5].append(k_rope[n_p:].reshape(dec_batch, t_len, dr))
    return (x[:n_p].reshape(batch, s_len, d), x[n_p:].reshape(dec_batch, t_len, d),
            jnp.stack(outs[0]), st_s_all, jnp.stack(outs[2]), jnp.stack(outs[3]),
            jnp.stack(outs[4]), jnp.stack(outs[5]))
```

```python
import functools
import math

import numpy as np
import jax
import jax.numpy as jnp
from jax import lax
from jax.experimental import pallas as pl
from jax.experimental.pallas import tpu as pltpu

F32 = jnp.float32
BF16 = jnp.bfloat16

GLA_GATE_TAU = 16.0
ROPE_THETA = 10000.0
N_GROUPS = 4
TOP_K = 2
LN_EPS = 1e-5
RMS_EPS = 1e-6

LANES = 128
VMEM_LIMIT = 56 * 1024 * 1024

GLA_CHUNK = 64
GLA_PROMPT_CHUNK = 128
SAMPLE_PAD = 8
EXPERT_TILE = 256


def _pick(n, prefs):
    for p in prefs:
        if p <= n and n % p == 0:
            return p
    return n


def _cparams(sem, vmem=VMEM_LIMIT):
    return pltpu.CompilerParams(dimension_semantics=sem, vmem_limit_bytes=vmem)


def _dot(a, b):
    return jnp.dot(a, b, preferred_element_type=F32)


def _dot_nt(a, b):
    return lax.dot_general(a, b, (((1,), (1,)), ((), ())), preferred_element_type=F32)


def _dot_tn(a, b):
    return lax.dot_general(a, b, (((0,), (0,)), ((), ())), preferred_element_type=F32)


def _split2(x):
    hi = x.astype(BF16)
    return hi, (x - hi.astype(F32)).astype(BF16)


def _silu(x):
    return x * (1.0 / (1.0 + jnp.exp(-x)))


def _layer_norm(y, g, b):
    mu = jnp.mean(y, axis=-1, keepdims=True)
    yc = y - mu
    var = jnp.mean(yc * yc, axis=-1, keepdims=True)
    return yc * lax.rsqrt(var + LN_EPS) * g + b


def _rms_norm(y, g):
    return y * lax.rsqrt(jnp.mean(y * y, axis=-1, keepdims=True) + RMS_EPS) * g


def _mm_slab_kernel(a_ref, b_ref, o_ref, acc_ref):
    k = pl.program_id(2)

    @pl.when(k == 0)
    def _():
        acc_ref[...] = jnp.zeros_like(acc_ref)

    acc_ref[...] += _dot_nt(a_ref[...], b_ref[...])

    @pl.when(k == pl.num_programs(2) - 1)
    def _():
        for j in range(o_ref.shape[0]):
            o_ref[j] = acc_ref[:, j * LANES:(j + 1) * LANES].astype(o_ref.dtype)


def _mm_slab(a, b):
    m, kd = a.shape
    n = b.shape[0]
    tm = _pick(m, (2176, 1088, 1024, 512, 256, 128, 64, 32, 16))
    tn = _pick(n, (1536, 1024, 512, 256, 128))
    tk = _pick(kd, (1024, 512, 256, 128))
    return pl.pallas_call(
        _mm_slab_kernel,
        grid=(n // tn, m // tm, kd // tk),
        in_specs=[pl.BlockSpec((tm, tk), lambda j, i, k: (i, k)),
                  pl.BlockSpec((tn, tk), lambda j, i, k: (j, k))],
        out_specs=pl.BlockSpec((tn // LANES, tm, LANES), lambda j, i, k: (j, i, 0)),
        out_shape=jax.ShapeDtypeStruct((n // LANES, m, LANES), BF16),
        scratch_shapes=[pltpu.VMEM((tm, tn), F32)],
        compiler_params=_cparams(("parallel", "parallel", "arbitrary")),
        name="in_proj_gla",
    )(a, b)


def _mla_in_kernel(a_ref, b_ref, cs_ref, qg_ref, kvg_ref,
                   cq_ref, ckv_ref, kr_ref, glr_ref, acc_ref, *, ql, kvl, dr):
    k = pl.program_id(1)

    @pl.when(k == 0)
    def _():
        acc_ref[...] = jnp.zeros_like(acc_ref)

    acc_ref[...] += _dot_nt(a_ref[...], b_ref[...])

    @pl.when(k == pl.num_programs(1) - 1)
    def _():
        cq = acc_ref[:, 0:ql]
        cq_ref[...] = _rms_norm(cq, qg_ref[...]).astype(cq_ref.dtype)
        ckv = acc_ref[:, ql:ql + kvl]
        ckv_ref[...] = _rms_norm(ckv, kvg_ref[...])
        u = acc_ref[:, ql + kvl:ql + kvl + LANES] * cs_ref[...]
        kr_ref[...] = (u + pltpu.roll(u, dr, 1))[:, 0:dr]
        glr_ref[...] = acc_ref[:, ql + kvl + LANES:ql + kvl + 2 * LANES].astype(glr_ref.dtype)


def _mla_in(xb, wm, cs, qg, kvg, ql, kvl, dr):
    m, kd = xb.shape
    n = wm.shape[0]
    tm = _pick(m, (512, 256, 128, 64, 32, 16))
    tk = _pick(kd, (1024, 512, 256, 128))
    kern = functools.partial(_mla_in_kernel, ql=ql, kvl=kvl, dr=dr)
    return pl.pallas_call(
        kern,
        grid=(m // tm, kd // tk),
        in_specs=[pl.BlockSpec((tm, tk), lambda i, k: (i, k)),
                  pl.BlockSpec((n, tk), lambda i, k: (0, k)),
                  pl.BlockSpec((tm, LANES), lambda i, k: (i, 0)),
                  pl.BlockSpec((1, ql), lambda i, k: (0, 0)),
                  pl.BlockSpec((1, kvl), lambda i, k: (0, 0))],
        out_specs=[pl.BlockSpec((tm, ql), lambda i, k: (i, 0)),
                   pl.BlockSpec((tm, kvl), lambda i, k: (i, 0)),
                   pl.BlockSpec((tm, dr), lambda i, k: (i, 0)),
                   pl.BlockSpec((tm, LANES), lambda i, k: (i, 0))],
        out_shape=[jax.ShapeDtypeStruct((m, ql), BF16),
                   jax.ShapeDtypeStruct((m, kvl), F32),
                   jax.ShapeDtypeStruct((m, dr), F32),
                   jax.ShapeDtypeStruct((m, LANES), BF16)],
        scratch_shapes=[pltpu.VMEM((tm, n), F32)],
        compiler_params=_cparams(("parallel", "arbitrary")),
        name="in_proj_mla",
    )(xb, wm, cs, qg, kvg)


def _q_up_kernel(cq_ref, w_ref, cs_ref, o_ref, *, dn, dr, scale):
    r = _dot(cq_ref[...], w_ref[...])
    o_ref[:, 0:dn] = (r[:, 0:dn] * scale).astype(o_ref.dtype)
    u = r[:, dn:dn + 2 * dr] * cs_ref[...]
    rot = u + pltpu.roll(u, dr, 1)
    o_ref[:, dn:dn + dr] = (rot[:, 0:dr] * scale).astype(o_ref.dtype)


def _q_up(cq, wq, cs, dn, dr, scale):
    m, ql = cq.shape
    h = wq.shape[0]
    tm = _pick(m, (1088, 1024, 512, 256, 128, 64, 32, 16))
    kern = functools.partial(_q_up_kernel, dn=dn, dr=dr, scale=scale)
    return pl.pallas_call(
        kern,
        grid=(m // tm, h),
        in_specs=[pl.BlockSpec((tm, ql), lambda i, j: (i, 0)),
                  pl.BlockSpec((None, ql, dn + 2 * dr), lambda i, j: (j, 0, 0)),
                  pl.BlockSpec((tm, LANES), lambda i, j: (i, 0))],
        out_specs=pl.BlockSpec((None, tm, dn + dr), lambda i, j: (j, i, 0)),
        out_shape=jax.ShapeDtypeStruct((h, m, dn + dr), BF16),
        compiler_params=_cparams(("parallel", "arbitrary")),
        name="mla_q_up",
    )(cq, wq, cs)


def _kv_up_kernel(c_ref, kr_ref, wk_ref, wv_ref, k_ref, v_ref, *, dn, dr):
    c = c_ref[...].astype(BF16)
    k_ref[:, 0:dn] = _dot(c, wk_ref[...].astype(BF16)).astype(k_ref.dtype)
    k_ref[:, dn:dn + dr] = kr_ref[...].astype(k_ref.dtype)
    v_ref[...] = _dot(c, wv_ref[...].astype(BF16)).astype(v_ref.dtype)


def _kv_up(ckv, kr, w_uk2, w_uv2, n_rows, h, dn, dv, dr):
    kvl = ckv.shape[1]
    tm = _pick(n_rows, (1024, 512, 256, 128, 64, 32, 16))
    kern = functools.partial(_kv_up_kernel, dn=dn, dr=dr)
    return pl.pallas_call(
        kern,
        grid=(n_rows // tm, h),
        in_specs=[pl.BlockSpec((tm, kvl), lambda i, j: (i, 0)),
                  pl.BlockSpec((tm, dr), lambda i, j: (i, 0)),
                  pl.BlockSpec((kvl, dn), lambda i, j: (0, j)),
                  pl.BlockSpec((kvl, dv), lambda i, j: (0, j))],
        out_specs=[pl.BlockSpec((None, tm, dn + dr), lambda i, j: (j, i, 0)),
                   pl.BlockSpec((None, tm, dv), lambda i, j: (j, i, 0))],
        out_shape=[jax.ShapeDtypeStruct((h, n_rows, dn + dr), BF16),
                   jax.ShapeDtypeStruct((h, n_rows, dv), BF16)],
        compiler_params=_cparams(("parallel", "arbitrary")),
        name="mla_kv_up",
    )(ckv, kr, w_uk2, w_uv2)


def _attn_prompt_kernel(q_ref, c_ref, kr_ref, wk_ref, wv_ref, o_ref, k_ref, v_ref, *, tq, dn):
    s_len = q_ref.shape[0]
    c = c_ref[...].astype(BF16)
    k_ref[:, 0:dn] = _dot(c, wk_ref[...].astype(BF16)).astype(k_ref.dtype)
    k_ref[:, dn:] = kr_ref[...].astype(k_ref.dtype)
    v_ref[...] = _dot(c, wv_ref[...].astype(BF16)).astype(v_ref.dtype)
    for i in range(s_len // tq):
        hi = (i + 1) * tq
        q = q_ref[i * tq:hi, :]
        s = _dot_nt(q, k_ref[0:hi, :])
        row = lax.broadcasted_iota(jnp.int32, s.shape, 0) + i * tq
        col = lax.broadcasted_iota(jnp.int32, s.shape, 1)
        s = jnp.where(col <= row, s, -jnp.inf)
        m = jnp.max(s, axis=-1, keepdims=True)
        p = jnp.exp(s - m)
        l = jnp.sum(p, axis=-1, keepdims=True)
        o = _dot(p.astype(BF16), v_ref[0:hi, :])
        o_ref[i * tq:hi, :] = (o / l).astype(o_ref.dtype)


def _attn_prompt(q3, ckv, kr, w_uk2, w_uv2, batch, s_len, dn, dv):
    h, _, dqk = q3.shape
    kvl = ckv.shape[1]
    dr = kr.shape[1]
    tq = _pick(s_len, (256, 128, 64, 32, 16))
    kern = functools.partial(_attn_prompt_kernel, tq=tq, dn=dn)
    return pl.pallas_call(
        kern,
        grid=(batch, h),
        in_specs=[pl.BlockSpec((None, s_len, dqk), lambda b, j: (j, b, 0)),
                  pl.BlockSpec((s_len, kvl), lambda b, j: (b, 0)),
                  pl.BlockSpec((s_len, dr), lambda b, j: (b, 0)),
                  pl.BlockSpec((kvl, dn), lambda b, j: (0, j)),
                  pl.BlockSpec((kvl, dv), lambda b, j: (0, j))],
        out_specs=pl.BlockSpec((None, s_len, dv), lambda b, j: (j, b, 0)),
        out_shape=jax.ShapeDtypeStruct((h, batch * s_len, dv), BF16),
        scratch_shapes=[pltpu.VMEM((s_len, dqk), BF16), pltpu.VMEM((s_len, dv), BF16)],
        compiler_params=_cparams(("parallel", "parallel")),
        name="mla_attn_prompt",
    )(q3, ckv, kr, w_uk2, w_uv2)


def _q_lat_kernel(q_ref, w_ref, o_ref, *, dn):
    o_ref[...] = _dot_nt(q_ref[:, 0:dn], w_ref[...].astype(BF16)).astype(o_ref.dtype)


def _q_lat(q3, w_uk2, row_block, n_rows, dn):
    h, _, dqk = q3.shape
    kvl = w_uk2.shape[0]
    kern = functools.partial(_q_lat_kernel, dn=dn)
    return pl.pallas_call(
        kern,
        grid=(h,),
        in_specs=[pl.BlockSpec((None, n_rows, dqk), lambda j: (j, row_block, 0)),
                  pl.BlockSpec((kvl, dn), lambda j: (0, j))],
        out_specs=pl.BlockSpec((None, n_rows, kvl), lambda j: (j, 0, 0)),
        out_shape=jax.ShapeDtypeStruct((h, n_rows, kvl), BF16),
        compiler_params=_cparams(("parallel",)),
        name="mla_q_absorb",
    )(q3, w_uk2)


def _o_uv_kernel(o_ref, w_ref, y_ref):
    y_ref[...] = _dot(o_ref[...], w_ref[...].astype(BF16)).astype(y_ref.dtype)


def _o_uv(o_lat3, w_uv2, dv):
    h, n_rows, kvl = o_lat3.shape
    return pl.pallas_call(
        _o_uv_kernel,
        grid=(h,),
        in_specs=[pl.BlockSpec((None, n_rows, kvl), lambda j: (j, 0, 0)),
                  pl.BlockSpec((kvl, dv), lambda j: (0, j))],
        out_specs=pl.BlockSpec((None, n_rows, dv), lambda j: (j, 0, 0)),
        out_shape=jax.ShapeDtypeStruct((h, n_rows, dv), BF16),
        compiler_params=_cparams(("parallel",)),
        name="mla_o_absorb",
    )(o_lat3, w_uv2)


def _attn_sample_kernel(pt_ref, ql_ref, qr_ref, cn_ref, kn_ref, cc_hbm, kc_hbm, o_ref,
                        cbuf, kbuf, sem, *, layer, n_pages, page, t_real):
    b = pl.program_id(0)
    nb = pl.num_programs(0)
    slot = b % 2

    def copies(bb, sl, p):
        pg = pt_ref[bb * n_pages + p]
        dst = pl.ds(p * page, page)
        return (pltpu.make_async_copy(cc_hbm.at[layer, pg], cbuf.at[sl, dst], sem.at[0, sl]),
                pltpu.make_async_copy(kc_hbm.at[layer, pg], kbuf.at[sl, :, dst], sem.at[1, sl]))

    def start(bb, sl):
        for p in range(n_pages):
            for c in copies(bb, sl, p):
                c.start()

    @pl.when(b == 0)
    def _():
        start(0, 0)

    @pl.when(b + 1 < nb)
    def _():
        start(b + 1, 1 - slot)

    for p in range(n_pages):
        for c in copies(b, slot, p):
            c.wait()

    cb = cbuf[slot].astype(BF16)
    kb = kbuf[slot].astype(BF16)
    ql = ql_ref[...]
    qr = qr_ref[...]
    cn = cn_ref[...].astype(BF16)
    kn = kn_ref[...].astype(BF16)
    s = _dot_nt(ql, cb) + _dot(qr, kb)
    sn = _dot_nt(ql, cn) + _dot_nt(qr, kn)
    t_of_row = lax.broadcasted_iota(jnp.int32, sn.shape, 0) % t_real
    j = lax.broadcasted_iota(jnp.int32, sn.shape, 1)
    sn = jnp.where(j <= t_of_row, sn, -jnp.inf)
    m = jnp.maximum(jnp.max(s, axis=-1, keepdims=True), jnp.max(sn, axis=-1, keepdims=True))
    p = jnp.exp(s - m)
    pn = jnp.exp(sn - m)
    l = jnp.sum(p, axis=-1, keepdims=True) + jnp.sum(pn, axis=-1, keepdims=True)
    o = _dot(p.astype(BF16), cb) + _dot(pn.astype(BF16), cn)
    o_ref[...] = (o / l).astype(o_ref.dtype)


def _attn_sample(page_table, qlat, qrope, c_new, k_new, cache_c, cache_k, layer, t_real):
    db, ht, kvl = qlat.shape
    dr = qrope.shape[2]
    n_pages = page_table.shape[1]
    page = cache_c.shape[2]
    past = n_pages * page
    tn = c_new.shape[1]
    assert cache_k.shape[2:] == (dr, page)
    kern = functools.partial(_attn_sample_kernel, layer=layer, n_pages=n_pages, page=page,
                             t_real=t_real)
    grid_spec = pltpu.PrefetchScalarGridSpec(
        num_scalar_prefetch=1,
        grid=(db,),
        in_specs=[pl.BlockSpec((None, ht, kvl), lambda b, pt: (b, 0, 0)),
                  pl.BlockSpec((None, ht, dr), lambda b, pt: (b, 0, 0)),
                  pl.BlockSpec((None, tn, kvl), lambda b, pt: (b, 0, 0)),
                  pl.BlockSpec((None, tn, dr), lambda b, pt: (b, 0, 0)),
                  pl.BlockSpec(memory_space=pl.ANY),
                  pl.BlockSpec(memory_space=pl.ANY)],
        out_specs=pl.BlockSpec((None, ht, kvl), lambda b, pt: (b, 0, 0)),
        scratch_shapes=[pltpu.VMEM((2, past, kvl), F32),
                        pltpu.VMEM((2, dr, past), F32),
                        pltpu.SemaphoreType.DMA((2, 2))],
    )
    return pl.pallas_call(
        kern,
        grid_spec=grid_spec,
        out_shape=jax.ShapeDtypeStruct((db, ht, kvl), BF16),
        compiler_params=_cparams(("arbitrary",)),
        name="mla_attn_sample",
    )(page_table.reshape(-1), qlat, qrope, c_new, k_new, cache_c, cache_k)


def _gla_constants(c, g, dv):
    nlev = int(math.log2(g))
    ng = c // g
    r = np.arange(c)
    t = r[:, None]
    j = r[None, :]
    same = (t // g) == (j // g)
    d = [same & (j <= t), same & (j > t)]
    m = [np.eye(c, dtype=bool)]
    for lvl in range(1, nlev + 1):
        size = 1 << lvl
        half = size >> 1
        blk = r // size
        mid = (blk * size + half - 1)[:, None]
        upper = ((r % size) >= half)[:, None]
        d.append(np.where(upper, (j > mid) & (j <= t), (j > t) & (j <= mid)))
        m.append((blk[:, None] == blk[None, :]) & upper & ~upper.T)
    col_group = np.arange(ng * dv)[None, :] // dv
    sel = (t == col_group * g + g - 1)
    bd = (t // g) == col_group
    return (jnp.asarray(np.concatenate(d, 0), BF16), jnp.asarray(np.stack(m), F32),
            jnp.asarray(sel, BF16), jnp.asarray(bd, BF16))


def _gla_chunk(q, k, v, go, glr, wgk, bgk, gn, dmat, m_ref, sel, bd, s_cat, *, c, g, t_real, dk):
    ng = c // g
    nlev = int(math.log2(g))
    dv = v.shape[1]
    x = _dot(glr, wgk) + bgk
    lg = (jnp.minimum(x, 0.0) - jnp.log1p(jnp.exp(-jnp.abs(x)))) * (1.0 / GLA_GATE_TAU)
    if t_real < g:
        rr = lax.broadcasted_iota(jnp.int32, lg.shape, 0) % g
        lg = jnp.where(rr < t_real, lg, 0.0)
    l2 = _split2(lg)
    ex = jnp.exp(_dot(dmat, l2[0]) + _dot(dmat, l2[1]))
    eb = ex[0:c]
    ekd = ex[c:2 * c]
    qf = q.astype(F32) * (dk ** -0.5)
    kf = k.astype(F32)
    attn = m_ref[0] * _dot_nt(qf.astype(BF16), k)
    for lvl in range(1, nlev + 1):
        e = ex[(1 + lvl) * c:(2 + lvl) * c]
        attn = attn + m_ref[lvl] * _dot_nt((qf * e).astype(BF16), (kf * e).astype(BF16))
    o = _dot(attn.astype(BF16), v)
    oi = _dot((qf * eb).astype(BF16), s_cat.astype(BF16))
    if ng == 1:
        o = o + oi
        vd = v
    else:
        o = o + jnp.concatenate(
            [oi[gi * g:(gi + 1) * g, gi * dv:(gi + 1) * dv] for gi in range(ng)], axis=0)
        vd = jnp.concatenate([v] * ng, axis=1) * bd
    e2 = _split2(eb)
    dec = _dot_tn(e2[0], sel) + _dot_tn(e2[1], sel)
    s_new = s_cat * dec + _dot_tn((kf * ekd).astype(BF16), vd)
    out = _rms_norm(o, gn) * _silu(go.astype(F32))
    return out, s_new


def _slabs(ref, rows):
    return jnp.concatenate([ref[j, rows, :] for j in range(ref.shape[0])], axis=1)


def _gla_prompt_kernel(q_ref, k_ref, v_ref, go_ref, glr_ref, wgk_ref, bgk_ref, gn_ref,
                       d_ref, m_ref, sel_ref, bd_ref, o_ref, s_ref, st_ref, *, c, dk):
    @pl.when(pl.program_id(2) == 0)
    def _():
        st_ref[...] = jnp.zeros_like(st_ref)

    state = st_ref[...]
    for i in range(q_ref.shape[0] // c):
        rows = slice(i * c, (i + 1) * c)
        out, state = _gla_chunk(
            q_ref[rows, :], k_ref[rows, :], _slabs(v_ref, rows), _slabs(go_ref, rows),
            glr_ref[rows, :], wgk_ref[...], bgk_ref[...], gn_ref[...], d_ref[...], m_ref,
            sel_ref[...], bd_ref[...], state, c=c, g=c, t_real=c, dk=dk)
        for j in range(o_ref.shape[0]):
            o_ref[j, rows, :] = out[:, j * LANES:(j + 1) * LANES].astype(o_ref.dtype)
    st_ref[...] = state
    s_ref[...] = state


def _gla_prompt(g3, glr, wgk, bgk, gn, consts, batch, s_len, hg, dk, dv):
    c = GLA_PROMPT_CHUNK
    nv = dv // LANES
    tb = _pick(s_len, (512, 256, 128))
    nb = s_len // tb
    dmat, mmat, sel, bd = consts
    kern = functools.partial(_gla_prompt_kernel, c=c, dk=dk)
    rowmap = lambda off: (lambda b, h, t: (off + h, b * nb + t, 0))
    const2 = lambda b, h, t: (0, 0)
    return pl.pallas_call(
        kern,
        grid=(batch, hg, nb),
        in_specs=[pl.BlockSpec((None, tb, LANES), rowmap(0)),
                  pl.BlockSpec((None, tb, LANES), rowmap(hg)),
                  pl.BlockSpec((nv, tb, LANES), rowmap(2 * hg // nv)),
                  pl.BlockSpec((nv, tb, LANES), rowmap(2 * hg // nv + hg)),
                  pl.BlockSpec((tb, LANES), lambda b, h, t: (b * nb + t, 0)),
                  pl.BlockSpec((LANES, dk), lambda b, h, t: (0, h)),
                  pl.BlockSpec((1, dk), lambda b, h, t: (0, h)),
                  pl.BlockSpec((1, dv), lambda b, h, t: (0, h)),
                  pl.BlockSpec(dmat.shape, const2),
                  pl.BlockSpec(mmat.shape, lambda b, h, t: (0, 0, 0)),
                  pl.BlockSpec(sel.shape, const2),
                  pl.BlockSpec(bd.shape, const2)],
        out_specs=[pl.BlockSpec((nv, tb, LANES), lambda b, h, t: (h, b * nb + t, 0)),
                   pl.BlockSpec((None, None, dk, dv), lambda b, h, t: (b, h, 0, 0))],
        out_shape=[jax.ShapeDtypeStruct((hg * nv, batch * s_len, LANES), BF16),
                   jax.ShapeDtypeStruct((batch, hg, dk, dv), F32)],
        scratch_shapes=[pltpu.VMEM((dk, dv), F32)],
        compiler_params=_cparams(("parallel", "parallel", "arbitrary")),
        name="gla_prompt",
    )(g3, g3, g3, g3, glr, wgk, bgk, gn, dmat, mmat, sel, bd)


def _gla_sample_kernel(q_ref, k_ref, v_ref, go_ref, glr_ref, wgk_ref, bgk_ref, gn_ref,
                       d_ref, m_ref, sel_ref, bd_ref, s0_ref, *rest, c, g, t_real, dk):
    o_ref, s_ref = rest[-2:]
    ng = c // g
    dv = s0_ref.shape[2]
    rows = slice(None)
    s_cat = jnp.concatenate([s0_ref[gi] for gi in range(ng)], axis=1)
    out, s_new = _gla_chunk(
        q_ref[...], k_ref[...], _slabs(v_ref, rows), _slabs(go_ref, rows), glr_ref[...],
        wgk_ref[...], bgk_ref[...], gn_ref[...], d_ref[...], m_ref, sel_ref[...], bd_ref[...],
        s_cat, c=c, g=g, t_real=t_real, dk=dk)
    for j in range(o_ref.shape[0]):
        o_ref[j] = out[:, j * LANES:(j + 1) * LANES].astype(o_ref.dtype)
    for gi in range(ng):
        s_ref[gi] = s_new[:, gi * dv:(gi + 1) * dv]


def _gla_sample(g3s, glr_s, wgk, bgk, gn, consts, state, new_states, layer, dec_batch, t_real,
                hg, dk, dv):
    c = GLA_CHUNK
    g = SAMPLE_PAD
    ng = c // g
    nv = dv // LANES
    dmat, mmat, sel, bd = consts
    kern = functools.partial(_gla_sample_kernel, c=c, g=g, t_real=t_real, dk=dk)
    rowmap = lambda off: (lambda i, h: (off + h, i, 0))
    const2 = lambda i, h: (0, 0)
    state_spec = pl.BlockSpec((None, ng, None, dk, dv), lambda i, h: (layer, i, h, 0, 0))
    extra_specs, extra_args, aliases = [], [], {}
    if new_states is not None:
        extra_specs, extra_args, aliases = [pl.BlockSpec(memory_space=pl.ANY)], [new_states], {13: 1}
    return pl.pallas_call(
        kern,
        grid=(dec_batch // ng, hg),
        in_specs=[pl.BlockSpec((None, c, LANES), rowmap(0)),
                  pl.BlockSpec((None, c, LANES), rowmap(hg)),
                  pl.BlockSpec((nv, c, LANES), rowmap(2 * hg // nv)),
                  pl.BlockSpec((nv, c, LANES), rowmap(2 * hg // nv + hg)),
                  pl.BlockSpec((c, LANES), lambda i, h: (i, 0)),
                  pl.BlockSpec((LANES, dk), lambda i, h: (0, h)),
                  pl.BlockSpec((1, dk), lambda i, h: (0, h)),
                  pl.BlockSpec((1, dv), lambda i, h: (0, h)),
                  pl.BlockSpec(dmat.shape, const2),
                  pl.BlockSpec(mmat.shape, lambda i, h: (0, 0, 0)),
                  pl.BlockSpec(sel.shape, const2),
                  pl.BlockSpec(bd.shape, const2),
                  state_spec] + extra_specs,
        out_specs=[pl.BlockSpec((nv, c, LANES), lambda i, h: (h, i, 0)), state_spec],
        out_shape=[jax.ShapeDtypeStruct((hg * nv, dec_batch * g, LANES), BF16),
                   jax.ShapeDtypeStruct(state.shape, F32)],
        input_output_aliases=aliases,
        compiler_params=_cparams(("parallel", "parallel")),
        name="gla_sample",
    )(g3s, g3s, g3s, g3s, glr_s, wgk, bgk, gn, dmat, mmat, sel, bd, state, *extra_args)


def _out_ln_kernel(gp_ref, gs_ref, mp_ref, ms_ref, b_ref, x_ref, g_ref, bb_ref, of_ref, acc_ref,
                   *, alpha, n_ip, nkg):
    i = pl.program_id(0)
    k = pl.program_id(1)

    @pl.when(k == 0)
    def _():
        acc_ref[...] = jnp.zeros_like(acc_ref)

    def accumulate(a_ref):
        a = jnp.concatenate([a_ref[j] for j in range(a_ref.shape[0])], axis=1)
        acc_ref[...] += _dot(a, b_ref[...])

    for is_s, is_m, ref in ((False, False, gp_ref), (True, False, gs_ref),
                            (False, True, mp_ref), (True, True, ms_ref)):
        @pl.when(((i >= n_ip) == is_s) & ((k >= nkg) == is_m))
        def _():
            accumulate(ref)

    @pl.when(k == pl.num_programs(1) - 1)
    def _():
        of_ref[...] = _layer_norm(alpha * x_ref[...] + acc_ref[...], g_ref[...], bb_ref[...])


def _out_ln(o_gla_p, o_gla_s, o_mla_p, o_mla_s, w_out, x, g, b, alpha):
    sg, n_p, _ = o_gla_p.shape
    sm = o_mla_p.shape[0]
    n_s = o_gla_s.shape[1]
    m, d = x.shape
    tm = _pick(math.gcd(n_p, n_s), (512, 256, 128, 64, 32, 16))
    tk = _pick(math.gcd(sg, sm) * LANES, (512, 256, 128))
    ts = tk // LANES
    n_ip = n_p // tm
    nkg, nkm = sg // ts, sm // ts

    def amap(sample, mla):
        def index(i, k):
            kk = jnp.clip(k - nkg, 0, nkm - 1) if mla else jnp.minimum(k, nkg - 1)
            if sample:
                return jnp.where(i >= n_ip, kk, 0), jnp.maximum(i - n_ip, 0), 0
            return jnp.where(i >= n_ip, 0, kk), jnp.minimum(i, n_ip - 1), 0
        return index

    kern = functools.partial(_out_ln_kernel, alpha=alpha, n_ip=n_ip, nkg=nkg)
    return pl.pallas_call(
        kern,
        grid=(m // tm, nkg + nkm),
        in_specs=[pl.BlockSpec((ts, tm, LANES), amap(False, False)),
                  pl.BlockSpec((ts, tm, LANES), amap(True, False)),
                  pl.BlockSpec((ts, tm, LANES), amap(False, True)),
                  pl.BlockSpec((ts, tm, LANES), amap(True, True)),
                  pl.BlockSpec((tk, d), lambda i, k: (k, 0)),
                  pl.BlockSpec((tm, d), lambda i, k: (i, 0), pipeline_mode=pl.Buffered(1)),
                  pl.BlockSpec((1, d), lambda i, k: (0, 0)),
                  pl.BlockSpec((1, d), lambda i, k: (0, 0))],
        out_specs=pl.BlockSpec((tm, d), lambda i, k: (i, 0)),
        out_shape=jax.ShapeDtypeStruct((m, d), F32),
        scratch_shapes=[pltpu.VMEM((tm, d), F32)],
        compiler_params=_cparams(("parallel", "arbitrary")),
        name="out_proj_ln",
    )(o_gla_p, o_gla_s, o_mla_p, o_mla_s, w_out, x, g, b)


def _first_max(vals):
    best = vals[0]
    idx = jnp.zeros(best.shape, jnp.int32)
    for j in range(1, len(vals)):
        upd = vals[j] > best
        idx = jnp.where(upd, j, idx)
        best = jnp.where(upd, vals[j], best)
    return best, idx


def _pick_row(vals, idx):
    out = vals[0]
    for j in range(1, len(vals)):
        out = jnp.where(idx == j, vals[j], out)
    return out


def _router_kernel(x_ref, wr_ref, bias_ref, tri_ref, eidx_ref, w_ref, rank_ref, cnt_ref,
                   run_ref, *, n_exp, n_groups):
    @pl.when(pl.program_id(0) == 0)
    def _():
        run_ref[...] = jnp.zeros_like(run_ref)

    epg = n_exp // n_groups
    logits = lax.dot_general(wr_ref[...], x_ref[...], (((1,), (1,)), ((), ())),
                             precision=lax.Precision.HIGHEST, preferred_element_type=F32)
    sc = 1.0 / (1.0 + jnp.exp(-logits))
    sel = sc + bias_ref[:, 0:1]
    sel_rows = [sel[e:e + 1, :] for e in range(n_exp)]
    sc_rows = [sc[e:e + 1, :] for e in range(n_exp)]
    group_scores = []
    for gi in range(n_groups):
        r = sel_rows[gi * epg:(gi + 1) * epg]
        best = None
        for a in range(epg):
            for b in range(a + 1, epg):
                s = r[a] + r[b]
                best = s if best is None else jnp.maximum(best, s)
        group_scores.append(best)
    _, g_idx = _first_max(group_scores)
    in_sel = [_pick_row([sel_rows[gi * epg + j] for gi in range(n_groups)], g_idx)
              for j in range(epg)]
    in_sc = [_pick_row([sc_rows[gi * epg + j] for gi in range(n_groups)], g_idx)
             for j in range(epg)]
    _, l1 = _first_max(in_sel)
    _, l2 = _first_max([jnp.where(l1 == j, -jnp.inf, in_sel[j]) for j in range(epg)])
    w1 = _pick_row(in_sc, l1)
    w2 = _pick_row(in_sc, l2)
    den = w1 + w2
    e1 = g_idx * epg + l1
    e2 = g_idx * epg + l2
    eiota = lax.broadcasted_iota(jnp.int32, sc.shape, 0)
    oh1 = (eiota == e1).astype(F32)
    oh2 = (eiota == e2).astype(F32)
    oh = oh1 + oh2
    before = run_ref[:, 0:1] + _dot(oh.astype(BF16), tri_ref[...])
    eidx_ref[0:1, :] = e1
    eidx_ref[1:2, :] = e2
    w_ref[0:1, :] = w1 / den
    w_ref[1:2, :] = w2 / den
    rank_ref[0:1, :] = jnp.sum(oh1 * before, axis=0, keepdims=True).astype(jnp.int32)
    rank_ref[1:2, :] = jnp.sum(oh2 * before, axis=0, keepdims=True).astype(jnp.int32)
    run_ref[...] += jnp.sum(oh, axis=1, keepdims=True)
    cnt_ref[...] = run_ref[...].astype(jnp.int32)


def _router(x, wr_t, bias, n_groups):
    m, d = x.shape
    n_exp = wr_t.shape[0]
    tm = _pick(m, (512, 256, 128))
    tri = jnp.asarray(np.triu(np.ones((tm, tm), np.float32), 1), BF16)
    kern = functools.partial(_router_kernel, n_exp=n_exp, n_groups=n_groups)
    return pl.pallas_call(
        kern,
        grid=(m // tm,),
        in_specs=[pl.BlockSpec((tm, d), lambda i: (i, 0)),
                  pl.BlockSpec((n_exp, d), lambda i: (0, 0)),
                  pl.BlockSpec((n_exp, LANES), lambda i: (0, 0)),
                  pl.BlockSpec((tm, tm), lambda i: (0, 0))],
        out_specs=[pl.BlockSpec((TOP_K, tm), lambda i: (0, i)),
                   pl.BlockSpec((TOP_K, tm), lambda i: (0, i)),
                   pl.BlockSpec((TOP_K, tm), lambda i: (0, i)),
                   pl.BlockSpec((n_exp, LANES), lambda i: (0, 0))],
        out_shape=[jax.ShapeDtypeStruct((TOP_K, m), jnp.int32),
                   jax.ShapeDtypeStruct((TOP_K, m), F32),
                   jax.ShapeDtypeStruct((TOP_K, m), jnp.int32),
                   jax.ShapeDtypeStruct((n_exp, LANES), jnp.int32)],
        scratch_shapes=[pltpu.VMEM((n_exp, LANES), F32)],
        compiler_params=_cparams(("arbitrary",)),
        name="moe_router",
    )(x, wr_t, bias, tri)


def _route_plan(eidx, rank, cnt, tile, n_tok):
    n_exp = cnt.shape[0]
    counts = cnt[:, 0]
    padded = ((counts + tile - 1) // tile) * tile
    ends = jnp.cumsum(padded)
    offs = ends - padded
    e_ids = jnp.arange(n_exp, dtype=jnp.int32)
    is_e = eidx[None] == e_ids[:, None, None]
    pos = (jnp.sum(jnp.where(is_e, offs[:, None, None], 0), axis=0) + rank).astype(jnp.int32)
    t_max = (TOP_K * n_tok) // tile + n_exp
    tiles = jnp.arange(t_max, dtype=jnp.int32)
    n_valid = (ends[-1] // tile).astype(jnp.int32)
    valid = tiles < n_valid
    row_block = jnp.where(valid, tiles, n_valid - 1)
    tile_exp = jnp.minimum(jnp.sum(ends[None, :] <= (row_block * tile)[:, None], axis=1),
                           n_exp - 1).astype(jnp.int32)
    tile_off = jnp.sum(jnp.where(tile_exp[:, None] == e_ids[None, :], offs[None, :], 0), axis=1)
    first = (valid & (tiles * tile == tile_off)).astype(jnp.int32)
    zero_row = jnp.where(padded > 0, ends - tile, -1).astype(jnp.int32)
    return pos.reshape(-1), tile_exp, row_block, first, n_valid.reshape(1), zero_row, t_max


def _pack_bf16_pairs(x):
    half = x.shape[1] // 2
    hi = lax.bitcast_convert_type(x[:, :half].astype(BF16).astype(F32), jnp.uint32)
    lo = lax.bitcast_convert_type(x[:, half:].astype(BF16).astype(F32), jnp.uint32)
    return hi | (lo >> 16)


def _unpack_bf16_pairs(p):
    hi = lax.bitcast_convert_type(p & jnp.uint32(0xFFFF0000), F32).astype(BF16)
    lo = lax.bitcast_convert_type(p << 16, F32).astype(BF16)
    return hi, lo


def _dispatch_kernel(pos_ref, zrow_ref, x_ref, xs_hbm, pbuf, zbuf, sem, zsem, *, tm, n_tok, tile,
                     n_exp):
    i = pl.program_id(0)
    pbuf[...] = _pack_bf16_pairs(x_ref[...])

    def zero_copy(e):
        r0 = pl.multiple_of(jnp.maximum(zrow_ref[e], 0), tile)
        return pltpu.make_async_copy(zbuf, xs_hbm.at[pl.ds(r0, tile)], zsem)

    @pl.when(i == 0)
    def _():
        zbuf[...] = jnp.zeros_like(zbuf)
        for e in range(n_exp):
            @pl.when(zrow_ref[e] >= 0)
            def _():
                zero_copy(e).start()
        for e in range(n_exp):
            @pl.when(zrow_ref[e] >= 0)
            def _():
                zero_copy(e).wait()

    def row_copy(r, s):
        p = pos_ref[s * n_tok + i * tm + r]
        return pltpu.make_async_copy(pbuf.at[pl.ds(r, 1)], xs_hbm.at[pl.ds(p, 1)], sem)

    def issue(r, carry):
        for s in range(TOP_K):
            row_copy(r, s).start()
        return carry

    def drain(r, carry):
        for s in range(TOP_K):
            row_copy(r, s).wait()
        return carry

    lax.fori_loop(0, tm, issue, 0, unroll=4)
    lax.fori_loop(0, tm, drain, 0, unroll=4)


def _dispatch(pos, zero_row, x, tile, n_rows):
    n_tok, d = x.shape
    n_exp = zero_row.shape[0]
    tm = _pick(n_tok, (256, 128, 64, 32, 16))
    kern = functools.partial(_dispatch_kernel, tm=tm, n_tok=n_tok, tile=tile, n_exp=n_exp)
    grid_spec = pltpu.PrefetchScalarGridSpec(
        num_scalar_prefetch=2,
        grid=(n_tok // tm,),
        in_specs=[pl.BlockSpec((tm, d), lambda i, p, z: (i, 0))],
        out_specs=pl.BlockSpec(memory_space=pl.ANY),
        scratch_shapes=[pltpu.VMEM((tm, d // 2), jnp.uint32),
                        pltpu.VMEM((tile, d // 2), jnp.uint32),
                        pltpu.SemaphoreType.DMA(()),
                        pltpu.SemaphoreType.DMA(())],
    )
    return pl.pallas_call(
        kern,
        grid_spec=grid_spec,
        out_shape=jax.ShapeDtypeStruct((n_rows, d // 2), jnp.uint32),
        compiler_params=_cparams(("arbitrary",)),
        name="moe_dispatch",
    )(pos, zero_row, x)


def _expert_up_kernel(te_ref, rb_ref, first_ref, nv_ref, x_ref, wg_ref, wu_ref, h_ref, wgb, wub):
    t = pl.program_id(1)

    @pl.when(t < nv_ref[0])
    def _():
        @pl.when(first_ref[t] == 1)
        def _():
            wgb[...] = wg_ref[...].astype(BF16)
            wub[...] = wu_ref[...].astype(BF16)

        x_hi, x_lo = _unpack_bf16_pairs(x_ref[...])
        half = x_ref.shape[1]
        g = _dot(x_hi, wgb[0:half, :]) + _dot(x_lo, wgb[half:, :])
        u = _dot(x_hi, wub[0:half, :]) + _dot(x_lo, wub[half:, :])
        h_ref[...] = (_silu(g) * u).astype(h_ref.dtype)


def _expert_up(plan, xs, w_gate, w_up, layer, tile):
    _, tile_exp, row_block, first, n_valid, _, t_max = plan
    n_rows = xs.shape[0]
    d = 2 * xs.shape[1]
    f = w_gate.shape[3]
    tf = _pick(f, (512, 256, 128))
    wmap = lambda j, t, te, rb, fi, nv: (layer, te[t], 0, j)
    grid_spec = pltpu.PrefetchScalarGridSpec(
        num_scalar_prefetch=4,
        grid=(f // tf, t_max),
        in_specs=[pl.BlockSpec((tile, d // 2), lambda j, t, te, rb, fi, nv: (rb[t], 0)),
                  pl.BlockSpec((None, None, d, tf), wmap),
                  pl.BlockSpec((None, None, d, tf), wmap)],
        out_specs=pl.BlockSpec((tile, tf), lambda j, t, te, rb, fi, nv: (rb[t], j)),
        scratch_shapes=[pltpu.VMEM((d, tf), BF16), pltpu.VMEM((d, tf), BF16)],
    )
    return pl.pallas_call(
        _expert_up_kernel,
        grid_spec=grid_spec,
        out_shape=jax.ShapeDtypeStruct((n_rows, f), BF16),
        compiler_params=_cparams(("arbitrary", "arbitrary"), 60 * 1024 * 1024),
        name="moe_expert_up",
    )(tile_exp, row_block, first, n_valid, xs, w_gate, w_up)


def _expert_down_kernel(te_ref, rb_ref, first_ref, nv_ref, h_ref, wd_ref, y_ref, wdb):
    t = pl.program_id(1)

    @pl.when(t < nv_ref[0])
    def _():
        @pl.when(first_ref[t] == 1)
        def _():
            wdb[...] = wd_ref[...].astype(BF16)

        y_ref[...] = _dot(h_ref[...], wdb[...])


def _expert_down(plan, hs, w_down, layer, tile):
    _, tile_exp, row_block, first, n_valid, _, t_max = plan
    n_rows, f = hs.shape
    d = w_down.shape[3]
    tn = _pick(d, (4096, 2048, 1024, 512, 256, 128))
    grid_spec = pltpu.PrefetchScalarGridSpec(
        num_scalar_prefetch=4,
        grid=(d // tn, t_max),
        in_specs=[pl.BlockSpec((tile, f), lambda j, t, te, rb, fi, nv: (rb[t], 0)),
                  pl.BlockSpec((None, None, f, tn), lambda j, t, te, rb, fi, nv: (layer, te[t], 0, j))],
        out_specs=pl.BlockSpec((tile, tn), lambda j, t, te, rb, fi, nv: (rb[t], j)),
        scratch_shapes=[pltpu.VMEM((f, tn), BF16)],
    )
    return pl.pallas_call(
        _expert_down_kernel,
        grid_spec=grid_spec,
        out_shape=jax.ShapeDtypeStruct((n_rows, d), F32),
        compiler_params=_cparams(("arbitrary", "arbitrary")),
        name="moe_expert_down",
    )(tile_exp, row_block, first, n_valid, hs, w_down)


def _combine_ln_kernel(pos_ref, x_ref, w_ref, g_ref, b_ref, ys_hbm, of_ref, ob_ref, buf, sem,
                       *, tm, n_tok, alpha):
    i = pl.program_id(0)
    slot = i % 2

    def row_copy(ii, sl, r, s):
        p = pos_ref[s * n_tok + ii * tm + r]
        return pltpu.make_async_copy(ys_hbm.at[pl.ds(p, 1)], buf.at[sl, s, pl.ds(r, 1)],
                                     sem.at[sl, s])

    def issue(ii, sl):
        def body(r, carry):
            for s in range(TOP_K):
                row_copy(ii, sl, r, s).start()
            return carry
        lax.fori_loop(0, tm, body, 0, unroll=4)

    def drain(ii, sl):
        def body(r, carry):
            for s in range(TOP_K):
                row_copy(ii, sl, r, s).wait()
            return carry
        lax.fori_loop(0, tm, body, 0, unroll=4)

    @pl.when(i == 0)
    def _():
        issue(0, 0)

    @pl.when(i + 1 < pl.num_programs(0))
    def _():
        issue(i + 1, 1 - slot)

    drain(i, slot)
    y = w_ref[:, 0:1] * buf[slot, 0] + w_ref[:, 1:2] * buf[slot, 1]
    z = _layer_norm(alpha * x_ref[...] + y, g_ref[...], b_ref[...])
    of_ref[...] = z
    ob_ref[...] = z.astype(ob_ref.dtype)


def _combine_ln(pos, x, wts, g, b, ys, alpha):
    m, d = x.shape
    tm = _pick(m, (256, 128, 64, 32, 16))
    kern = functools.partial(_combine_ln_kernel, tm=tm, n_tok=m, alpha=alpha)
    grid_spec = pltpu.PrefetchScalarGridSpec(
        num_scalar_prefetch=1,
        grid=(m // tm,),
        in_specs=[pl.BlockSpec((tm, d), lambda i, p: (i, 0)),
                  pl.BlockSpec((tm, TOP_K), lambda i, p: (i, 0)),
                  pl.BlockSpec((1, d), lambda i, p: (0, 0)),
                  pl.BlockSpec((1, d), lambda i, p: (0, 0)),
                  pl.BlockSpec(memory_space=pl.ANY)],
        out_specs=[pl.BlockSpec((tm, d), lambda i, p: (i, 0)),
                   pl.BlockSpec((tm, d), lambda i, p: (i, 0))],
        scratch_shapes=[pltpu.VMEM((2, TOP_K, tm, d), F32),
                        pltpu.SemaphoreType.DMA((2, TOP_K))],
    )
    return pl.pallas_call(
        kern,
        grid_spec=grid_spec,
        out_shape=[jax.ShapeDtypeStruct((m, d), F32), jax.ShapeDtypeStruct((m, d), BF16)],
        compiler_params=_cparams(("arbitrary",)),
        name="moe_combine_ln",
    )(pos, x, wts, g, b, ys)


def _rope_table(pos, dr):
    half = dr // 2
    inv_freq = ROPE_THETA ** (-(jnp.arange(half, dtype=F32) * 2.0) / dr)
    ang = pos.astype(F32)[:, None] * inv_freq[None, :]
    cos, sin = jnp.cos(ang), jnp.sin(ang)
    return jnp.concatenate([cos, cos, sin, sin], axis=-1)


def _rotate_half_cols(w, dr):
    half = dr // 2
    return jnp.concatenate([-w[..., half:], w[..., :half]], axis=-1)


def kernel(x_prompt, x_sample, cache_kv_latent, cache_k_rope, state_gla, page_table, w_in, w_gk_up, b_gk, gla_norm_g, q_norm_g, kv_norm_g, w_q_up, w_uk, w_uv, w_out, ln1_g, ln1_b, ln2_g, ln2_b, w_router, router_bias, w_gate, w_up, w_down):
    batch, s_len, d = x_prompt.shape
    dec_batch, t_len, _ = x_sample.shape
    depth = w_in.shape[0]
    _, _, hg, dk, dv = state_gla.shape
    rank = w_gk_up.shape[1]
    ql = q_norm_g.shape[1]
    kvl = kv_norm_g.shape[1]
    hm, dn = w_uk.shape[2], w_uk.shape[3]
    dvm = w_uv.shape[3]
    dr = w_q_up.shape[3] - dn
    n_exp = w_router.shape[1]
    page = cache_kv_latent.shape[2]
    past_len = page_table.shape[1] * page
    n_p = batch * s_len
    n_s = dec_batch * t_len
    n_tok = n_p + n_s
    assert dk == LANES and dv % LANES == 0 and 2 * dr == LANES and dn == LANES and dvm == LANES
    assert rank <= LANES and n_p % n_s == 0 and t_len <= SAMPLE_PAD
    alpha = (2.0 * depth) ** 0.25
    mla_scale = (dn + dr) ** -0.5
    hk, hv = hg * dk, hg * dv
    nv = dv // LANES

    o_glr = 2 * hk + hv
    o_gout = o_glr + rank
    o_cq = o_gout + hv
    o_kr = o_cq + ql + kvl
    w_in_t = jnp.swapaxes(w_in, 1, 2)
    w_gla = jnp.concatenate([w_in_t[:, :o_glr], w_in_t[:, o_gout:o_cq]], axis=1).astype(BF16)
    w_kr = w_in_t[:, o_kr:o_kr + dr]
    w_kr_rot = jnp.concatenate([-w_kr[:, dr // 2:], w_kr[:, :dr // 2]], axis=1)
    w_mla = jnp.concatenate(
        [w_in_t[:, o_cq:o_kr], w_kr, w_kr_rot, w_in_t[:, o_glr:o_gout],
         jnp.zeros((depth, LANES - rank, d), F32)], axis=1).astype(BF16)
    wgk = jnp.concatenate([w_gk_up, jnp.zeros((depth, LANES - rank, hk), F32)], axis=1).astype(BF16)
    wq_rope = w_q_up[..., dn:]
    wq = jnp.concatenate([w_q_up, _rotate_half_cols(wq_rope, dr)], axis=-1)
    wq = jnp.transpose(wq, (0, 2, 1, 3)).astype(BF16)
    w_uk2 = w_uk.reshape(depth, kvl, hm * dn)
    w_uv2 = w_uv.reshape(depth, kvl, hm * dvm)
    w_out_b = w_out.astype(BF16)
    wr_t = jnp.transpose(w_router).astype(F32)
    bias_b = jnp.broadcast_to(router_bias.astype(F32)[:, None], (n_exp, LANES))
    pos_rows = jnp.concatenate([jnp.tile(jnp.arange(s_len), batch),
                                jnp.tile(past_len + jnp.arange(t_len), dec_batch)])
    cs = _rope_table(pos_rows, dr)
    consts_p = _gla_constants(GLA_PROMPT_CHUNK, GLA_PROMPT_CHUNK, dv)
    consts_s = _gla_constants(GLA_CHUNK, SAMPLE_PAD, dv)
    new_pad = 16
    cache_k_t = jnp.swapaxes(cache_k_rope, 2, 3)

    x = jnp.concatenate([x_prompt.reshape(n_p, d), x_sample.reshape(n_s, d)], axis=0)
    xb = x.astype(BF16)
    outs = [[] for _ in range(6)]
    st_s_all = None
    for l in range(depth):
        g3 = _mm_slab(xb, w_gla[l])
        cq_n, ckv_n, k_rope, glr = _mla_in(xb, w_mla[l], cs, q_norm_g[l][None], kv_norm_g[l][None],
                                           ql, kvl, dr)
        bgk_l = b_gk[l][None]
        gn_l = gla_norm_g[l][None]
        o_gla_p, st_p = _gla_prompt(g3, glr, wgk[l], bgk_l, gn_l, consts_p, batch, s_len, hg, dk, dv)
        pad_t = ((0, 0), (0, 0), (0, SAMPLE_PAD - t_len), (0, 0))
        g3s = jnp.pad(g3[:, n_p:, :].reshape(-1, dec_batch, t_len, LANES), pad_t)
        g3s = g3s.reshape(-1, dec_batch * SAMPLE_PAD, LANES)
        glr_s = jnp.pad(glr[n_p:].reshape(dec_batch, t_len, LANES), pad_t[1:])
        glr_s = glr_s.reshape(dec_batch * SAMPLE_PAD, LANES)
        o_gla_s, st_s_all = _gla_sample(g3s, glr_s, wgk[l], bgk_l, gn_l, consts_s, state_gla,
                                        st_s_all, l, dec_batch, t_len, hg, dk, dv)
        o_gla_s = o_gla_s.reshape(hg * nv, dec_batch, SAMPLE_PAD, LANES)[:, :, :t_len]
        o_gla_s = o_gla_s.reshape(hg * nv, n_s, LANES)
        q3 = _q_up(cq_n, wq[l], cs, dn, dr, mla_scale)
        o_mla_p = _attn_prompt(q3, ckv_n, k_rope, w_uk2[l], w_uv2[l], batch, s_len, dn, dvm)
        qlat = _q_lat(q3, w_uk2[l], n_p // n_s, n_s, dn)

        def per_seq(a):
            w = a.shape[-1]
            a = a.reshape(hm, dec_batch, t_len, w)
            return jnp.transpose(a, (1, 0, 2, 3)).reshape(dec_batch, hm * t_len, w)

        pad_n = ((0, 0), (0, new_pad - t_len), (0, 0))
        c_new = jnp.pad(ckv_n[n_p:].reshape(dec_batch, t_len, kvl), pad_n)
        k_new = jnp.pad(k_rope[n_p:].reshape(dec_batch, t_len, dr), pad_n)
        o_lat = _attn_sample(page_table, per_seq(qlat), per_seq(q3[:, n_p:, dn:]), c_new, k_new,
                             cache_kv_latent, cache_k_t, l, t_len)
        o_lat3 = jnp.transpose(o_lat.reshape(dec_batch, hm, t_len, kvl), (1, 0, 2, 3))
        o_mla_s = _o_uv(o_lat3.reshape(hm, n_s, kvl), w_uv2[l], dvm)
        x1 = _out_ln(o_gla_p, o_gla_s, o_mla_p, o_mla_s, w_out_b[l], x, ln1_g[l][None],
                     ln1_b[l][None], alpha)
        eidx, wts, rank_in, cnt = _router(x1, wr_t, bias_b, N_GROUPS)
        plan = _route_plan(eidx, rank_in, cnt, EXPERT_TILE, n_tok)
        xs = _dispatch(plan[0], plan[5], x1, EXPERT_TILE, plan[6] * EXPERT_TILE)
        hs = _expert_up(plan, xs, w_gate, w_up, l, EXPERT_TILE)
        ys = _expert_down(plan, hs, w_down, l, EXPERT_TILE)
        x, xb = _combine_ln(plan[0], x1, jnp.transpose(wts), ln2_g[l][None], ln2_b[l][None], ys, alpha)
        outs[0].append(st_p)
        outs[2].append(ckv_n[:n_p].reshape(batch, s_len, kvl))
        outs[3].append(k_rope[:n_p].reshape(batch, s_len, dr))
        outs[4].append(ckv_n[n_p:].reshape(dec_batch, t_len, kvl))
        outs[5].append(k_rope[n_p:].reshape(dec_batch, t_len, dr))
    return (x[:n_p].reshape(batch, s_len, d), x[n_p:].reshape(dec_batch, t_len, d),
            jnp.stack(outs[0]), st_s_all, jnp.stack(outs[2]), jnp.stack(outs[3]),
            jnp.stack(outs[4]), jnp.stack(outs[5]))
```
